```python
import math
import jax
import jax.numpy as jnp
from jax import lax
import numpy as np

D_MODEL = 2048
BATCH = 4
SEQ = 2048
DEPTH = 2
DEC_BATCH = 128
DEC_SEQ = 1
PAST_LEN = 2048
PAGE_SIZE = 128

D_ATTN = D_MODEL // 2
D_SSD = D_MODEL - D_ATTN
DA_HEAD_QK = 64
DA_HEAD_V = 2 * DA_HEAD_QK
N_HEADS_A = D_ATTN // DA_HEAD_V
SSD_HEAD_DIM = 64
N_HEADS_S = D_SSD // SSD_HEAD_DIM
SSD_GROUPS = 2
SSD_HPG = N_HEADS_S // SSD_GROUPS
SSD_STATE = 128
CONV_K = 4
CONV_DIM = D_SSD + 2 * SSD_GROUPS * SSD_STATE
SSD_CHUNK = 128
IN_SPLITS = (D_ATTN, 2 * D_ATTN, 3 * D_ATTN, 3 * D_ATTN + D_SSD, 3 * D_ATTN + D_SSD + CONV_DIM)
N_IN = 3 * D_ATTN + D_SSD + CONV_DIM + N_HEADS_S
N_MEM = 256
N_HEADS_MEM = 4
D_MEM_HEAD = D_MODEL // N_HEADS_MEM
D_FF = 5632
N_EXPERTS = 8
TOP_K = 2
D_EXP = 7168
Q_BLOCK = 128
N_DENSE = (DEPTH + 1) // 2
N_MOE = DEPTH // 2
DEEPNORM_ALPHA = (2 * DEPTH) ** 0.25
DEEPNORM_BETA = (8 * DEPTH) ** -0.25
LN_EPS = 1e-5
RMS_EPS = 1e-5

kernel_name = 'hybrid_diffattn_ssd_decoder'


def layer_norm(x, g, b):
    xf = x.astype(jnp.float32)
    mu = jnp.mean(xf, -1, keepdims=True)
    var = jnp.mean(jnp.square(xf - mu), -1, keepdims=True)
    return ((xf - mu) * lax.rsqrt(var + LN_EPS) * g.astype(jnp.float32) + b.astype(jnp.float32)).astype(x.dtype)


def rms_norm(x, g):
    xf = x.astype(jnp.float32)
    y = xf * lax.rsqrt(jnp.mean(jnp.square(xf), -1, keepdims=True) + RMS_EPS)
    return (y * g.astype(jnp.float32)).astype(x.dtype)


def alibi_slopes(n):
    return 2.0 ** (-(8.0 / n) * jnp.arange(1, n + 1, dtype=jnp.float32))


def _diff_attn_block(q, k, v, q_pos, k_pos, lam):
    s = jnp.einsum('bqhmd,bshmd->bhmqs', q, k).astype(jnp.float32) * (DA_HEAD_QK ** -0.5)
    dist = (q_pos[:, None] - k_pos[None, :]).astype(jnp.float32)
    s = s - alibi_slopes(N_HEADS_A)[None, :, None, None, None] * dist
    s = jnp.where(k_pos[None, :] <= q_pos[:, None], s, -jnp.inf)
    pr = jax.nn.softmax(s, axis=-1)
    w = pr[:, :, 0] - lam * pr[:, :, 1]
    return jnp.einsum('bhqs,bshd->bqhd', w.astype(v.dtype), v)


def diff_attention(q, k, v, q_pos, k_pos, lam):
    b, Q = q.shape[0], q.shape[1]
    if Q > Q_BLOCK and Q % Q_BLOCK == 0:
        nb = Q // Q_BLOCK
        qb = q.reshape(b, nb, Q_BLOCK, N_HEADS_A, 2, DA_HEAD_QK).swapaxes(0, 1)
        pb = q_pos.reshape(nb, Q_BLOCK)
        out = lax.map(lambda a: _diff_attn_block(a[0], k, v, a[1], k_pos, lam), (qb, pb))
        return out.swapaxes(0, 1).reshape(b, Q, N_HEADS_A, DA_HEAD_V)
    return _diff_attn_block(q, k, v, q_pos, k_pos, lam)


def ssd_scan(x, dt, A, Bm, Cm, h0):
    b, L, G, R, P = x.shape
    N = Bm.shape[-1]
    Q = SSD_CHUNK if L % SSD_CHUNK == 0 else L
    nc = L // Q
    f32 = jnp.float32
    x = x.astype(f32).reshape(b, nc, Q, G, R, P)
    dt = dt.astype(f32).reshape(b, nc, Q, G, R)
    Bm = Bm.astype(f32).reshape(b, nc, Q, G, N)
    Cm = Cm.astype(f32).reshape(b, nc, Q, G, N)
    a_cs = jnp.cumsum(dt * A, axis=2)
    seg = a_cs[:, :, :, None] - a_cs[:, :, None, :]
    causal = jnp.tril(jnp.ones((Q, Q), dtype=bool))[None, None, :, :, None, None]
    decay = jnp.exp(jnp.where(causal, seg, -jnp.inf))
    cb = jnp.einsum('bcign,bcjgn->bcijg', Cm, Bm)
    xdt = x * dt[..., None]
    y_diag = jnp.einsum('bcijgr,bcjgrp->bcigrp', cb[..., None] * decay, xdt)
    decay_end = jnp.exp(a_cs[:, :, -1:] - a_cs)
    chunk_states = jnp.einsum('bcjgn,bcjgrp->bcgrpn', Bm, xdt * decay_end[..., None])
    chunk_decay = jnp.exp(a_cs[:, :, -1])

    def step(h, inp):
        st, dcy = inp
        return dcy[..., None, None] * h + st, h

    h_final, h_prev = lax.scan(step, h0.astype(f32),
                               (chunk_states.swapaxes(0, 1), chunk_decay.swapaxes(0, 1)))
    h_prev = h_prev.swapaxes(0, 1)
    y_off = jnp.einsum('bcign,bcgrpn->bcigrp', Cm, h_prev) * jnp.exp(a_cs)[..., None]
    return (y_diag + y_off).reshape(b, L, G, R, P), h_final


def token_mixer(h, layer, p, conv_buf, ssm_h0, past_k, past_v, pos0):
    b, L, _ = h.shape
    u = jnp.einsum('bld,de->ble', h, p['w_in'][layer])
    q, k, v, z, xbc, dt_raw = jnp.split(u, IN_SPLITS, axis=-1)
    q = q.reshape(b, L, N_HEADS_A, 2, DA_HEAD_QK)
    k = k.reshape(b, L, N_HEADS_A, 2, DA_HEAD_QK)
    v = v.reshape(b, L, N_HEADS_A, DA_HEAD_V)
    lam_init = 0.8 - 0.6 * math.exp(-0.3 * layer)
    lp = p['lam_params'][layer].astype(jnp.float32)
    lam = jnp.exp(jnp.sum(lp[0] * lp[1])) - jnp.exp(jnp.sum(lp[2] * lp[3])) + lam_init
    if past_k is None:
        k_all, v_all = k, v
    else:
        k_all = jnp.concatenate([past_k.astype(k.dtype), k], axis=1)
        v_all = jnp.concatenate([past_v.astype(v.dtype), v], axis=1)
    q_pos = pos0 + jnp.arange(L, dtype=jnp.int32)
    k_pos = jnp.arange(k_all.shape[1], dtype=jnp.int32)
    o_a = diff_attention(q, k_all, v_all, q_pos, k_pos, lam)
    o_a = (rms_norm(o_a, p['subln_g'][layer]) * (1.0 - lam_init)).reshape(b, L, D_ATTN)
    xpad = jnp.concatenate([conv_buf.astype(xbc.dtype), xbc], axis=1)
    new_conv = xpad[:, -(CONV_K - 1):]
    cw = p['conv_w'][layer][:, None, :].astype(xpad.dtype)
    xc = lax.conv_general_dilated(xpad, cw, window_strides=(1,), padding='VALID',
                                  dimension_numbers=('NWC', 'WIO', 'NWC'), feature_group_count=CONV_DIM)
    xc = jax.nn.silu(xc + p['conv_b'][layer])
    xs = xc[..., :D_SSD].reshape(b, L, SSD_GROUPS, SSD_HPG, SSD_HEAD_DIM)
    Bm = xc[..., D_SSD:D_SSD + SSD_GROUPS * SSD_STATE].reshape(b, L, SSD_GROUPS, SSD_STATE)
    Cm = xc[..., D_SSD + SSD_GROUPS * SSD_STATE:].reshape(b, L, SSD_GROUPS, SSD_STATE)
    dt = jax.nn.softplus(dt_raw.astype(jnp.float32) + p['dt_bias'][layer].astype(jnp.float32))
    dt = dt.reshape(b, L, SSD_GROUPS, SSD_HPG)
    A = -jnp.exp(p['a_log'][layer].astype(jnp.float32)).reshape(SSD_GROUPS, SSD_HPG)
    h0 = ssm_h0.reshape(b, SSD_GROUPS, SSD_HPG, SSD_HEAD_DIM, SSD_STATE)
    y, h_new = ssd_scan(xs, dt, A, Bm, Cm, h0)
    dsk = p['d_skip'][layer].astype(jnp.float32).reshape(SSD_GROUPS, SSD_HPG)[..., None]
    y = (y + dsk * xs.astype(jnp.float32)).reshape(b, L, D_SSD)
    g = (y * jax.nn.silu(z.astype(jnp.float32))).reshape(b, L, SSD_GROUPS, D_SSD // SSD_GROUPS)
    g = g * lax.rsqrt(jnp.mean(jnp.square(g), -1, keepdims=True) + RMS_EPS)
    y_s = (g.reshape(b, L, D_SSD) * p['ssd_norm_g'][layer].astype(jnp.float32)).astype(o_a.dtype)
    out = jnp.einsum('ble,ed->bld', jnp.concatenate([o_a, y_s], axis=-1), p['w_out'][layer])
    k_row = k.reshape(b, L, N_HEADS_A, 2 * DA_HEAD_QK)
    h_new = h_new.reshape(b, N_HEADS_S, SSD_HEAD_DIM, SSD_STATE).astype(h.dtype)
    return out, k_row, v, h_new, new_conv


def memory_kv(mem, w_kv):
    b, M, _ = mem.shape
    kv = jnp.einsum('bmd,de->bme', mem, w_kv)
    k, v = jnp.split(kv, 2, axis=-1)
    return k.reshape(b, M, N_HEADS_MEM, D_MEM_HEAD), v.reshape(b, M, N_HEADS_MEM, D_MEM_HEAD)


def memory_attention(h, mk, mv, w_q, w_o):
    b, L, _ = h.shape
    q = jnp.einsum('bld,de->ble', h, w_q).reshape(b, L, N_HEADS_MEM, D_MEM_HEAD)
    s = jnp.einsum('bqhd,bkhd->bhqk', q, mk.astype(q.dtype)).astype(jnp.float32) * (D_MEM_HEAD ** -0.5)
    pr = jax.nn.softmax(s, axis=-1)
    o = jnp.einsum('bhqk,bkhd->bqhd', pr.astype(q.dtype), mv.astype(q.dtype)).reshape(b, L, D_MODEL)
    return jnp.einsum('ble,ed->bld', o, w_o)


def swiglu(h, wg, wu, wd):
    a = jax.nn.silu(jnp.einsum('bld,df->blf', h, wg)) * jnp.einsum('bld,df->blf', h, wu)
    return jnp.einsum('blf,fd->bld', a, wd)


def moe_swiglu(h, w_router, wg, wu, wd):
    logits = jnp.einsum('bld,de->ble', h, w_router).astype(jnp.float32)
    top_v, top_i = lax.top_k(logits, TOP_K)
    gates = jax.nn.softmax(top_v, axis=-1)
    comb = jnp.sum(jax.nn.one_hot(top_i, N_EXPERTS, dtype=jnp.float32) * gates[..., None], axis=-2)
    out = jnp.zeros_like(h)
    for e in range(N_EXPERTS):
        out = out + comb[..., e:e + 1].astype(h.dtype) * swiglu(h, wg[e], wu[e], wd[e])
    return out


def trunk_layer(x, layer, p, mk, mv, conv_buf, ssm_h0, past_k, past_v, pos0):
    a, k_new, v_new, h_new, conv_new = token_mixer(x, layer, p, conv_buf, ssm_h0, past_k, past_v, pos0)
    x = layer_norm(DEEPNORM_ALPHA * x + a, p['ln_g'][layer, 0], p['ln_b'][layer, 0])
    c = memory_attention(x, mk, mv, p['w_mem_q'][layer], p['w_mem_o'][layer])
    x = layer_norm(DEEPNORM_ALPHA * x + c, p['ln_g'][layer, 1], p['ln_b'][layer, 1])
    if layer % 2 == 0:
        i = layer // 2
        f = swiglu(x, p['w_ff_gate'][i], p['w_ff_up'][i], p['w_ff_down'][i])
    else:
        i = layer // 2
        f = moe_swiglu(x, p['w_router'][i], p['w_exp_gate'][i], p['w_exp_up'][i], p['w_exp_down'][i])
    x = layer_norm(DEEPNORM_ALPHA * x + f, p['ln_g'][layer, 2], p['ln_b'][layer, 2])
    return x, k_new, v_new, h_new, conv_new


def setup_inputs(seed: int = 0) -> dict:
    key = jax.random.key(seed)
    ks = iter(jax.random.split(key, 48))
    f32 = jnp.float32

    def nrm(shape, scale):
        return jax.random.normal(next(ks), shape, f32) * scale

    n_pages = PAST_LEN // PAGE_SIZE
    n_used = DEC_BATCH * n_pages
    n_phys = n_used + max(1, n_used // 4)
    d_in = D_MODEL ** -0.5
    out = {}
    out['x_prompt'] = nrm((BATCH, SEQ, D_MODEL), 1.0)
    out['x_sample'] = nrm((DEC_BATCH, DEC_SEQ, D_MODEL), 1.0)
    out['mem_prompt'] = nrm((BATCH, N_MEM, D_MODEL), 1.0)
    out['cache_k'] = nrm((DEPTH, n_phys, PAGE_SIZE, N_HEADS_A, 2 * DA_HEAD_QK), 1.0)
    out['cache_v'] = nrm((DEPTH, n_phys, PAGE_SIZE, N_HEADS_A, DA_HEAD_V), 1.0)
    out['cache_mem_k'] = nrm((DEPTH, DEC_BATCH, N_MEM, N_HEADS_MEM, D_MEM_HEAD), 1.0)
    out['cache_mem_v'] = nrm((DEPTH, DEC_BATCH, N_MEM, N_HEADS_MEM, D_MEM_HEAD), 1.0)
    out['state_ssm'] = nrm((DEPTH, DEC_BATCH, N_HEADS_S, SSD_HEAD_DIM, SSD_STATE), 0.3)
    out['state_conv'] = nrm((DEPTH, DEC_BATCH, CONV_K - 1, CONV_DIM), 1.0)
    perm = jax.random.permutation(next(ks), n_phys)[:n_used]
    out['page_table'] = perm.reshape(DEC_BATCH, n_pages).astype(jnp.int32)
    out['ln_in_g'] = 1.0 + nrm((D_MODEL,), 0.05)
    out['ln_in_b'] = nrm((D_MODEL,), 0.02)
    w_in = nrm((DEPTH, D_MODEL, N_IN), d_in)
    out['w_in'] = w_in.at[:, :, 2 * D_ATTN:3 * D_ATTN].multiply(DEEPNORM_BETA)
    out['conv_w'] = nrm((DEPTH, CONV_K, CONV_DIM), CONV_K ** -0.5)
    out['conv_b'] = nrm((DEPTH, CONV_DIM), 0.02)
    dt0 = jnp.exp(jax.random.uniform(next(ks), (DEPTH, N_HEADS_S), f32, math.log(1e-3), math.log(1e-1)))
    out['dt_bias'] = dt0 + jnp.log(-jnp.expm1(-dt0))
    out['a_log'] = jnp.log(jax.random.uniform(next(ks), (DEPTH, N_HEADS_S), f32, 1.0, 16.0))
    out['d_skip'] = 1.0 + nrm((DEPTH, N_HEADS_S), 0.1)
    out['ssd_norm_g'] = 1.0 + nrm((DEPTH, D_SSD), 0.05)
    out['lam_params'] = nrm((DEPTH, 4, DA_HEAD_QK), 0.1)
    out['subln_g'] = 1.0 + nrm((DEPTH, DA_HEAD_V), 0.05)
    out['w_out'] = nrm((DEPTH, D_MODEL, D_MODEL), d_in * DEEPNORM_BETA)
    out['w_mem_q'] = nrm((DEPTH, D_MODEL, D_MODEL), d_in)
    w_kv = nrm((DEPTH, D_MODEL, 2 * D_MODEL), d_in)
    out['w_mem_kv'] = w_kv.at[:, :, D_MODEL:].multiply(DEEPNORM_BETA)
    out['w_mem_o'] = nrm((DEPTH, D_MODEL, D_MODEL), d_in * DEEPNORM_BETA)
    out['ln_g'] = 1.0 + nrm((DEPTH, 3, D_MODEL), 0.05)
    out['ln_b'] = nrm((DEPTH, 3, D_MODEL), 0.02)
    out['w_ff_gate'] = nrm((N_DENSE, D_MODEL, D_FF), d_in)
    out['w_ff_up'] = nrm((N_DENSE, D_MODEL, D_FF), d_in)
    out['w_ff_down'] = nrm((N_DENSE, D_FF, D_MODEL), D_FF ** -0.5 * DEEPNORM_BETA)
    out['w_router'] = nrm((N_MOE, D_MODEL, N_EXPERTS), d_in)
    out['w_exp_gate'] = nrm((N_MOE, N_EXPERTS, D_MODEL, D_EXP), d_in)
    out['w_exp_up'] = nrm((N_MOE, N_EXPERTS, D_MODEL, D_EXP), d_in)
    out['w_exp_down'] = nrm((N_MOE, N_EXPERTS, D_EXP, D_MODEL), D_EXP ** -0.5 * DEEPNORM_BETA)
    return out


def reference(x_prompt, x_sample, mem_prompt, cache_k, cache_v, cache_mem_k, cache_mem_v, state_ssm,
              state_conv, page_table, ln_in_g, ln_in_b, w_in, conv_w, conv_b, dt_bias, a_log, d_skip,
              ssd_norm_g, lam_params, subln_g, w_out, w_mem_q, w_mem_kv, w_mem_o, ln_g, ln_b,
              w_ff_gate, w_ff_up, w_ff_down, w_router, w_exp_gate, w_exp_up, w_exp_down):
    p = dict(w_in=w_in, conv_w=conv_w, conv_b=conv_b, dt_bias=dt_bias, a_log=a_log, d_skip=d_skip,
             ssd_norm_g=ssd_norm_g, lam_params=lam_params, subln_g=subln_g, w_out=w_out,
             w_mem_q=w_mem_q, w_mem_o=w_mem_o, ln_g=ln_g, ln_b=ln_b, w_ff_gate=w_ff_gate,
             w_ff_up=w_ff_up, w_ff_down=w_ff_down, w_router=w_router, w_exp_gate=w_exp_gate,
             w_exp_up=w_exp_up, w_exp_down=w_exp_down)
    b = x_prompt.shape[0]
    h = layer_norm(x_prompt, ln_in_g, ln_in_b)
    kp, vp, sp, cp, mkp, mvp = [], [], [], [], [], []
    for l in range(DEPTH):
        mk, mv = memory_kv(mem_prompt, w_mem_kv[l])
        conv0 = jnp.zeros((b, CONV_K - 1, CONV_DIM), h.dtype)
        ssm0 = jnp.zeros((b, N_HEADS_S, SSD_HEAD_DIM, SSD_STATE), jnp.float32)
        h, k_new, v_new, s_new, c_new = trunk_layer(h, l, p, mk, mv, conv0, ssm0, None, None, 0)
        kp.append(k_new); vp.append(v_new); sp.append(s_new); cp.append(c_new); mkp.append(mk); mvp.append(mv)
    y_prompt = h
    db = x_sample.shape[0]
    n_pages = page_table.shape[1]
    past_len = n_pages * PAGE_SIZE
    hs = layer_norm(x_sample, ln_in_g, ln_in_b)
    ks_, vs_, ss_, cs_ = [], [], [], []
    for l in range(DEPTH):
        past_k = cache_k[l, page_table].reshape(db, past_len, N_HEADS_A, 2, DA_HEAD_QK)
        past_v = cache_v[l, page_table].reshape(db, past_len, N_HEADS_A, DA_HEAD_V)
        hs, k_new, v_new, s_new, c_new = trunk_layer(hs, l, p, cache_mem_k[l], cache_mem_v[l], state_conv[l],
                                                     state_ssm[l], past_k, past_v, past_len)
        ks_.append(k_new); vs_.append(v_new); ss_.append(s_new); cs_.append(c_new)
    y_sample = hs
    return (y_prompt, y_sample,
            jnp.stack(kp), jnp.stack(vp), jnp.stack(sp), jnp.stack(cp), jnp.stack(mkp), jnp.stack(mvp),
            jnp.stack(ks_), jnp.stack(vs_), jnp.stack(ss_), jnp.stack(cs_))
```

```python
import functools
import math

import jax
import jax.numpy as jnp
from jax import lax
from jax.experimental import pallas as pl
from jax.experimental.pallas import tpu as pltpu

F32 = jnp.float32
BF16 = jnp.bfloat16

LN_EPS = 1e-5
RMS_EPS = 1e-5
N_HEADS_A = 8
QK_DIM = 64
N_HEADS_S = 16
SSD_GROUPS = 2
SSD_STATE = 128
SSD_CHUNK = 128
CONV_K = 4
N_HEADS_MEM = 4
TOP_K = 2
VMEM_LIMIT = 56 * 1024 * 1024


def _cparams(*sem):
    return pltpu.CompilerParams(dimension_semantics=sem, vmem_limit_bytes=VMEM_LIMIT)


def _row_block(m, target):
    best = None
    for d in range(16, min(m, target) + 1, 16):
        if m % d == 0:
            best = d
    assert best is not None, (m, target)
    return best


def _ln_rows(x, g, b):
    mu = jnp.mean(x, axis=-1, keepdims=True)
    xc = x - mu
    var = jnp.mean(xc * xc, axis=-1, keepdims=True)
    return xc * lax.rsqrt(var + LN_EPS) * g + b


def _ln_kernel(x_ref, g_ref, b_ref, o_ref, obf_ref):
    y = _ln_rows(x_ref[...], g_ref[...], b_ref[...])
    o_ref[...] = y
    obf_ref[...] = y.astype(BF16)


def layer_norm_in(x, g, b):
    m, d = x.shape
    bm = _row_block(m, 512)
    row = pl.BlockSpec((bm, d), lambda i: (i, 0))
    vec = pl.BlockSpec((1, d), lambda i: (0, 0))
    return pl.pallas_call(
        _ln_kernel, grid=(m // bm,), in_specs=[row, vec, vec], out_specs=[row, row],
        out_shape=[jax.ShapeDtypeStruct((m, d), F32), jax.ShapeDtypeStruct((m, d), BF16)],
        compiler_params=_cparams("parallel"), name="ln_in",
    )(x, g.reshape(1, d), b.reshape(1, d))


def _add_ln_kernel(alpha, x_ref, a_ref, g_ref, b_ref, o_ref, obf_ref):
    y = _ln_rows(alpha * x_ref[...] + a_ref[...], g_ref[...], b_ref[...])
    o_ref[...] = y
    obf_ref[...] = y.astype(BF16)


def add_layer_norm(x, a, g, b, alpha):
    m, d = x.shape
    bm = _row_block(m, 512)
    row = pl.BlockSpec((bm, d), lambda i: (i, 0))
    vec = pl.BlockSpec((1, d), lambda i: (0, 0))
    return pl.pallas_call(
        functools.partial(_add_ln_kernel, alpha), grid=(m // bm,),
        in_specs=[row, row, vec, vec], out_specs=[row, row],
        out_shape=[jax.ShapeDtypeStruct((m, d), F32), jax.ShapeDtypeStruct((m, d), BF16)],
        compiler_params=_cparams("parallel"), name="add_ln",
    )(x, a, g.reshape(1, d), b.reshape(1, d))


def _mm_kernel(x_ref, w_ref, o_ref, wbf_ref):
    @pl.when(pl.program_id(1) == 0)
    def _():
        wbf_ref[...] = w_ref[...].astype(BF16)

    o_ref[...] = jnp.dot(x_ref[...].astype(BF16), wbf_ref[...],
                         preferred_element_type=F32).astype(o_ref.dtype)


def matmul(x, w, *, lead=(), col0=0, ncols=None, bm, bn, out_dtype=F32, name="mm"):
    m, k = x.shape
    n = w.shape[-1] - col0 if ncols is None else ncols
    assert w.shape[-2] == k and m % bm == 0 and n % bn == 0 and col0 % bn == 0
    nl = len(lead)
    cb0 = col0 // bn
    w_spec = pl.BlockSpec((None,) * nl + (k, bn), lambda j, i: tuple(lead) + (0, cb0 + j))
    return pl.pallas_call(
        _mm_kernel, grid=(n // bn, m // bm),
        in_specs=[pl.BlockSpec((bm, k), lambda j, i: (i, 0)), w_spec],
        out_specs=pl.BlockSpec((bm, bn), lambda j, i: (i, j)),
        out_shape=jax.ShapeDtypeStruct((m, n), out_dtype),
        scratch_shapes=[pltpu.VMEM((k, bn), BF16)],
        compiler_params=_cparams("parallel", "arbitrary"), name=name,
    )(x, w)


def _swiglu_up_kernel(x_ref, wg_ref, wu_ref, o_ref, wgbf_ref, wubf_ref):
    @pl.when(pl.program_id(1) == 0)
    def _():
        wgbf_ref[...] = wg_ref[...].astype(BF16)
        wubf_ref[...] = wu_ref[...].astype(BF16)

    x = x_ref[...]
    g = jnp.dot(x, wgbf_ref[...], preferred_element_type=F32)
    u = jnp.dot(x, wubf_ref[...], preferred_element_type=F32)
    o_ref[...] = (g * jax.nn.sigmoid(g) * u).astype(o_ref.dtype)


def swiglu_up(x, wg, wu, *, lead, bm, bn):
    m, k = x.shape
    n = wg.shape[-1]
    assert m % bm == 0 and n % bn == 0
    nl = len(lead)
    w_spec = pl.BlockSpec((None,) * nl + (k, bn), lambda j, i: tuple(lead) + (0, j))
    return pl.pallas_call(
        _swiglu_up_kernel, grid=(n // bn, m // bm),
        in_specs=[pl.BlockSpec((bm, k), lambda j, i: (i, 0)), w_spec, w_spec],
        out_specs=pl.BlockSpec((bm, bn), lambda j, i: (i, j)),
        out_shape=jax.ShapeDtypeStruct((m, n), BF16),
        scratch_shapes=[pltpu.VMEM((k, bn), BF16), pltpu.VMEM((k, bn), BF16)],
        compiler_params=_cparams("parallel", "arbitrary"), name="swiglu_up",
    )(x, wg, wu)


def _lambda_value(lp, lam_init):
    t1 = jnp.sum(lp[0:1, :] * lp[1:2, :], axis=1, keepdims=True)
    t2 = jnp.sum(lp[2:3, :] * lp[3:4, :], axis=1, keepdims=True)
    return jnp.exp(t1) - jnp.exp(t2) + lam_init


def _attn_prompt_kernel(tq, lam_init, q_ref, k_ref, v_ref, lp_ref, g_ref, o_ref, kbf_ref, vbf_ref):
    h = pl.program_id(1)
    qi = pl.program_id(2)

    @pl.when(qi == 0)
    def _():
        kbf_ref[...] = k_ref[...].astype(BF16)
        vbf_ref[...] = v_ref[...].astype(BF16)

    slope = jnp.exp2(-(h + 1).astype(F32) * jnp.ones((1, 1), F32))
    lam = _lambda_value(lp_ref[...], lam_init)
    q = q_ref[...] * (QK_DIM ** -0.5)
    lane = lax.broadcasted_iota(jnp.int32, q.shape, 1)
    q1 = jnp.where(lane < QK_DIM, q, 0.0).astype(BF16)
    q2 = jnp.where(lane >= QK_DIM, q, 0.0).astype(BF16)
    row = lax.broadcasted_iota(jnp.int32, (tq, tq), 0)
    col = lax.broadcasted_iota(jnp.int32, (tq, tq), 1)
    base = -slope * (row - col).astype(F32)
    dims = (((1,), (1,)), ((), ()))

    def chunk(kj, carry, masked):
        start = pl.multiple_of(kj * tq, tq)
        kc = kbf_ref[pl.ds(start, tq), :]
        vc = vbf_ref[pl.ds(start, tq), :]
        off = -slope * ((qi - kj) * tq).astype(F32)
        out = []
        for qm, (m, l, acc) in zip((q1, q2), carry):
            t = lax.dot_general(qm, kc, dims, preferred_element_type=F32) + base
            if masked:
                t = jnp.where(col <= row, t, -jnp.inf)
            m_new = jnp.maximum(m, jnp.max(t, axis=1, keepdims=True) + off)
            p = jnp.exp(t - (m_new - off))
            alpha = jnp.exp(m - m_new)
            l_new = alpha * l + jnp.sum(p, axis=1, keepdims=True)
            acc_new = alpha * acc + jnp.dot(p.astype(BF16), vc, preferred_element_type=F32)
            out.append((m_new, l_new, acc_new))
        return tuple(out)

    init_one = (jnp.full((tq, 1), -1e30, F32), jnp.zeros((tq, 1), F32), jnp.zeros((tq, 128), F32))
    carry = lax.fori_loop(0, qi, lambda kj, c: chunk(kj, c, False), (init_one, init_one))
    (_, l1, a1), (_, l2, a2) = chunk(qi, carry, True)
    o = a1 / l1 - lam * (a2 / l2)
    o = o * lax.rsqrt(jnp.mean(o * o, axis=1, keepdims=True) + RMS_EPS)
    o_ref[...] = (o * g_ref[...] * (1.0 - lam_init)).astype(o_ref.dtype)


def attn_prompt(u, lam_params, subln_g, layer, batch, seq, lam_init, tq=256):
    nq = seq // tq
    lp_spec = pl.BlockSpec((None, 4, QK_DIM), lambda b, h, i: (layer, 0, 0))
    g_spec = pl.BlockSpec((None, 1, 128), lambda b, h, i: (layer, 0, 0))
    return pl.pallas_call(
        functools.partial(_attn_prompt_kernel, tq, lam_init),
        grid=(batch, N_HEADS_A, nq),
        in_specs=[pl.BlockSpec((tq, 128), lambda b, h, i: (b * nq + i, h)),
                  pl.BlockSpec((seq, 128), lambda b, h, i: (b, N_HEADS_A + h)),
                  pl.BlockSpec((seq, 128), lambda b, h, i: (b, 2 * N_HEADS_A + h)),
                  lp_spec, g_spec],
        out_specs=pl.BlockSpec((tq, 128), lambda b, h, i: (b * nq + i, h)),
        out_shape=jax.ShapeDtypeStruct((batch * seq, N_HEADS_A * 128), BF16),
        scratch_shapes=[pltpu.VMEM((seq, 128), BF16), pltpu.VMEM((seq, 128), BF16)],
        compiler_params=_cparams("parallel", "parallel", "arbitrary"), name="attn_prompt",
    )(u, u, u, lam_params, subln_g.reshape(-1, 1, 128))


def _softplus(x):
    return jnp.maximum(x, 0.0) + jnp.log1p(jnp.exp(-jnp.abs(x)))


def _silu(x):
    return x * jax.nn.sigmoid(x)


def _ssd_prompt_kernel(xs_ref, bc_ref, z_ref, dt_ref, cw_ref, cb_ref, dtb_ref, alog_ref, dsk_ref, ng_ref,
                       y_ref, st_ref, conv_ref, xp_ref, h_ref):
    c = pl.program_id(1)
    nc = pl.num_programs(1)
    q = SSD_CHUNK
    d_ssd = N_HEADS_S * 64

    @pl.when(c == 0)
    def _():
        xp_ref[0:8, :] = jnp.zeros((8, xp_ref.shape[1]), F32)
        h_ref[...] = jnp.zeros_like(h_ref)

    xp_ref[8:8 + q, 0:d_ssd] = xs_ref[...]
    xp_ref[8:8 + q, d_ssd:] = bc_ref[...]
    cw = cw_ref[...]
    xc = cb_ref[...] + cw[3:4, :] * xp_ref[8:8 + q, :]
    for j in range(1, CONV_K):
        xc = xc + cw[3 - j:4 - j, :] * xp_ref[8 - j:8 - j + q, :]
    xp_ref[0:8, :] = xp_ref[q:q + 8, :]
    xc = _silu(xc)
    xs = xc[:, :d_ssd]

    dt = _softplus(dt_ref[...] + dtb_ref[...])
    a_neg = -jnp.exp(alog_ref[...])
    da = dt * a_neg
    ri = lax.broadcasted_iota(jnp.int32, (q, q), 0)
    ci = lax.broadcasted_iota(jnp.int32, (q, q), 1)
    causal = ci <= ri
    tril = jnp.where(causal, 1.0, 0.0).astype(F32)
    a_cs = jnp.dot(tril, da, preferred_element_type=F32, precision=lax.Precision.HIGHEST)
    a_cs_t = a_cs.T
    a_last = a_cs[q - 1:q, :]
    e_cs = jnp.exp(a_cs)
    e_end = jnp.exp(a_last - a_cs)
    e_last = jnp.exp(a_last)
    lane = lax.broadcasted_iota(jnp.int32, (q, 128), 1)
    lo = lane < 64
    rsel = lax.broadcasted_iota(jnp.int32, (128, SSD_STATE), 0) < 64
    dims_nt = (((1,), (1,)), ((), ()))
    dims_tn = (((0,), (0,)), ((), ()))
    hpg = N_HEADS_S // SSD_GROUPS

    ys = []
    for g in range(SSD_GROUPS):
        bm_g = xc[:, d_ssd + g * SSD_STATE:d_ssd + (g + 1) * SSD_STATE].astype(BF16)
        cm_g = xc[:, d_ssd + (SSD_GROUPS + g) * SSD_STATE:d_ssd + (SSD_GROUPS + g + 1) * SSD_STATE].astype(BF16)
        cb = lax.dot_general(cm_g, bm_g, dims_nt, preferred_element_type=F32)
        for pr in range(hpg // 2):
            h0 = g * hpg + 2 * pr
            x_pair = xs[:, h0 * 64:h0 * 64 + 128]
            dt_pair = jnp.where(lo, dt[:, h0:h0 + 1], dt[:, h0 + 1:h0 + 2])
            xdt = x_pair * dt_pair
            y_pair = jnp.zeros((q, 128), F32)
            for k, keep in ((0, lo), (1, jnp.logical_not(lo))):
                hh = h0 + k
                seg = a_cs[:, hh:hh + 1] - a_cs_t[hh:hh + 1, :]
                decay = jnp.exp(jnp.where(causal, seg, -jnp.inf))
                mat = (cb * decay).astype(BF16)
                y_pair = y_pair + jnp.dot(mat, jnp.where(keep, xdt, 0.0).astype(BF16), preferred_element_type=F32)
            end_pair = jnp.where(lo, e_end[:, h0:h0 + 1], e_end[:, h0 + 1:h0 + 2])
            cs_pair = jnp.where(lo, e_cs[:, h0:h0 + 1], e_cs[:, h0 + 1:h0 + 2])
            h_prev = h_ref[h0 * 64:h0 * 64 + 128, :]
            y_off = lax.dot_general(cm_g, h_prev.astype(BF16), dims_nt, preferred_element_type=F32) * cs_pair
            st = lax.dot_general((xdt * end_pair).astype(BF16), bm_g, dims_tn, preferred_element_type=F32)
            dec = jnp.where(rsel, e_last[:, h0:h0 + 1], e_last[:, h0 + 1:h0 + 2])
            h_ref[h0 * 64:h0 * 64 + 128, :] = dec * h_prev + st
            dsk_pair = jnp.where(lo[0:1, :], dsk_ref[:, h0:h0 + 1], dsk_ref[:, h0 + 1:h0 + 2])
            ys.append(y_pair + y_off + dsk_pair * x_pair)
    y = jnp.concatenate(ys, axis=1)
    gz = y * _silu(z_ref[...])
    half = d_ssd // SSD_GROUPS
    outs = []
    for g in range(SSD_GROUPS):
        part = gz[:, g * half:(g + 1) * half]
        outs.append(part * lax.rsqrt(jnp.mean(part * part, axis=1, keepdims=True) + RMS_EPS))
    y_ref[...] = (jnp.concatenate(outs, axis=1) * ng_ref[...]).astype(y_ref.dtype)

    @pl.when(c == nc - 1)
    def _():
        st_ref[...] = h_ref[...]
        conv_ref[:, 0:d_ssd] = xs_ref[q - 8:q, :]
        conv_ref[:, d_ssd:] = bc_ref[q - 8:q, :]


def _pad_lanes(v):
    return jnp.pad(v.reshape(1, -1), ((0, 0), (0, 128 - v.shape[-1])))


def ssd_prompt(u, dt_raw, conv_w, conv_b, dt_bias, a_log, d_skip, norm_g, layer, batch, seq):
    q = SSD_CHUNK
    nc = seq // q
    d_ssd = N_HEADS_S * 64
    d_bc = 2 * SSD_GROUPS * SSD_STATE
    cdim = d_ssd + d_bc
    d_attn = N_HEADS_A * 128
    z0, x0, bc0 = 3 * d_attn, 3 * d_attn + d_ssd, 3 * d_attn + 2 * d_ssd
    assert z0 % d_ssd == 0 and x0 % d_ssd == 0 and bc0 % d_bc == 0
    row = lambda b, c: (b * nc + c, 0)
    vec = lambda width: pl.BlockSpec((1, width), lambda b, c: (0, 0))
    return pl.pallas_call(
        _ssd_prompt_kernel, grid=(batch, nc),
        in_specs=[pl.BlockSpec((q, d_ssd), lambda b, c: (b * nc + c, x0 // d_ssd)),
                  pl.BlockSpec((q, d_bc), lambda b, c: (b * nc + c, bc0 // d_bc)),
                  pl.BlockSpec((q, d_ssd), lambda b, c: (b * nc + c, z0 // d_ssd)),
                  pl.BlockSpec((q, 128), row),
                  pl.BlockSpec((None, CONV_K, cdim), lambda b, c: (layer, 0, 0)),
                  vec(cdim), vec(128), vec(128), vec(128), vec(d_ssd)],
        out_specs=[pl.BlockSpec((q, d_ssd), row),
                   pl.BlockSpec((None, N_HEADS_S * 64, SSD_STATE), lambda b, c: (b, 0, 0)),
                   pl.BlockSpec((None, 8, cdim), lambda b, c: (b, 0, 0))],
        out_shape=[jax.ShapeDtypeStruct((batch * seq, d_ssd), BF16),
                   jax.ShapeDtypeStruct((batch, N_HEADS_S * 64, SSD_STATE), F32),
                   jax.ShapeDtypeStruct((batch, 8, cdim), F32)],
        scratch_shapes=[pltpu.VMEM((q + 8, cdim), F32), pltpu.VMEM((N_HEADS_S * 64, SSD_STATE), F32)],
        compiler_params=_cparams("parallel", "arbitrary"), name="ssd_prompt",
    )(u, u, u, dt_raw, conv_w, conv_b[layer].reshape(1, -1), _pad_lanes(dt_bias[layer]),
      _pad_lanes(a_log[layer]), _pad_lanes(d_skip[layer]), norm_g[layer].reshape(1, -1))


def _attn_decode_kernel(n_pages, page, lam_init, pt_ref, q_ref, kn_ref, vn_ref, lp_ref, g_ref, *refs):
    k_refs, v_refs, o_ref = refs[:n_pages], refs[n_pages:2 * n_pages], refs[-1]
    lam = _lambda_value(lp_ref[...], lam_init)
    q = q_ref[...] * (QK_DIM ** -0.5)
    lane = lax.broadcasted_iota(jnp.int32, q.shape, 1)
    qs = (jnp.where(lane < QK_DIM, q, 0.0), jnp.where(lane >= QK_DIM, q, 0.0))
    slope = jnp.exp2(-(lax.broadcasted_iota(jnp.int32, (1, N_HEADS_A, 1), 1) + 1).astype(F32))
    tok = lax.broadcasted_iota(jnp.int32, (page, 1, 1), 0)
    past = n_pages * page

    s_pages = [[], []]
    for p in range(n_pages):
        kp = k_refs[p][...]
        bias = slope * (past - p * page - tok).astype(F32)
        for mi in range(2):
            s_pages[mi].append(jnp.sum(kp * qs[mi][None], axis=-1, keepdims=True) - bias)
    kn = kn_ref[...]
    acc_terms = []
    for mi in range(2):
        s_new = jnp.sum(kn * qs[mi], axis=-1, keepdims=True)
        m = functools.reduce(jnp.maximum, [jnp.max(s, axis=0) for s in s_pages[mi]] + [s_new])
        e_pages = [jnp.exp(s - m[None]) for s in s_pages[mi]]
        e_new = jnp.exp(s_new - m)
        l = functools.reduce(jnp.add, [jnp.sum(e, axis=0) for e in e_pages] + [e_new])
        acc_terms.append((e_pages, e_new, 1.0 / l))
    (e1, e1n, r1), (e2, e2n, r2) = acc_terms
    c2 = lam * r2
    acc = (e1n * r1 - e2n * c2) * vn_ref[...]
    for p in range(n_pages):
        w = e1[p] * r1[None] - e2[p] * c2[None]
        acc = acc + jnp.sum(w * v_refs[p][...], axis=0)
    o = acc * lax.rsqrt(jnp.mean(acc * acc, axis=-1, keepdims=True) + RMS_EPS)
    o_ref[...] = o * g_ref[...] * (1.0 - lam_init)


def attn_decode(q, k_new, v_new, cache_k, cache_v, page_table, lam_params, subln_g, layer, lam_init):
    db, n_pages = page_table.shape
    page = cache_k.shape[2]

    def kv_spec(j):
        return pl.BlockSpec((None, None, page, N_HEADS_A, 128), lambda b, pt: (layer, pt[b, j], 0, 0, 0))

    tok = pl.BlockSpec((None, N_HEADS_A, 128), lambda b, pt: (b, 0, 0))
    in_specs = [tok, tok, tok,
                pl.BlockSpec((None, 4, QK_DIM), lambda b, pt: (layer, 0, 0)),
                pl.BlockSpec((None, 1, 128), lambda b, pt: (layer, 0, 0))]
    in_specs += [kv_spec(j) for j in range(n_pages)] * 2
    return pl.pallas_call(
        functools.partial(_attn_decode_kernel, n_pages, page, lam_init),
        grid_spec=pltpu.PrefetchScalarGridSpec(num_scalar_prefetch=1, grid=(db,), in_specs=in_specs, out_specs=tok),
        out_shape=jax.ShapeDtypeStruct((db, N_HEADS_A, 128), F32),
        compiler_params=_cparams("parallel"), name="attn_decode",
    )(page_table, q, k_new, v_new, lam_params, subln_g.reshape(-1, 1, 128), *([cache_k] * n_pages),
      *([cache_v] * n_pages))


def _ssd_decode_kernel(xbc_ref, z_ref, dt_ref, cst_ref, ssm_ref, cw_ref, cb_ref, dtb_ref, alog_ref, dsk_ref,
                       ng_ref, y_ref, ssm_out_ref, conv_out_ref):
    d_ssd = N_HEADS_S * 64
    hpg = N_HEADS_S // SSD_GROUPS
    xnew = xbc_ref[...]
    cst = cst_ref[...]
    cw = cw_ref[...]
    xc = cb_ref[...] + cw[3:4, :] * xnew
    for j in range(CONV_K - 1):
        xc = xc + cw[j:j + 1, :] * cst[j:j + 1, :]
    xc = _silu(xc)
    conv_out_ref[0:2, :] = cst[1:3, :]
    conv_out_ref[2:3, :] = xnew

    dt = _softplus(dt_ref[...] + dtb_ref[...])
    dec = jnp.exp(dt * (-jnp.exp(alog_ref[...])))
    eye = lax.broadcasted_iota(jnp.int32, (128, 128), 0) == lax.broadcasted_iota(jnp.int32, (128, 128), 1)
    lo = lax.broadcasted_iota(jnp.int32, (1, 128), 1) < 64
    rsel = lax.broadcasted_iota(jnp.int32, (128, 1), 0) < 64
    ys = []
    for pr in range(N_HEADS_S // 2):
        h0 = 2 * pr
        g = h0 // hpg
        x_pair = xc[:, h0 * 64:h0 * 64 + 128]
        b_g = xc[:, d_ssd + g * SSD_STATE:d_ssd + (g + 1) * SSD_STATE]
        c_g = xc[:, d_ssd + (SSD_GROUPS + g) * SSD_STATE:d_ssd + (SSD_GROUPS + g + 1) * SSD_STATE]
        dt_pair = jnp.where(lo, dt[:, h0:h0 + 1], dt[:, h0 + 1:h0 + 2])
        xdt_row = x_pair * dt_pair
        xdt_col = jnp.sum(jnp.where(eye, jnp.broadcast_to(xdt_row, (128, 128)), 0.0), axis=1, keepdims=True)
        h_prev = ssm_ref[h0:h0 + 2].reshape(128, SSD_STATE)
        dec_col = jnp.where(rsel, dec[:, h0:h0 + 1], dec[:, h0 + 1:h0 + 2])
        h_new = dec_col * h_prev + xdt_col * b_g
        ssm_out_ref[h0:h0 + 2] = h_new.reshape(2, 64, SSD_STATE)
        y_col = jnp.sum(h_new * c_g, axis=1, keepdims=True)
        y_row = jnp.sum(jnp.where(eye, jnp.broadcast_to(y_col, (128, 128)), 0.0), axis=0, keepdims=True)
        dsk_pair = jnp.where(lo, dsk_ref[:, h0:h0 + 1], dsk_ref[:, h0 + 1:h0 + 2])
        ys.append(y_row + dsk_pair * x_pair)
    y = jnp.concatenate(ys, axis=1)
    gz = y * _silu(z_ref[...])
    half = d_ssd // SSD_GROUPS
    outs = []
    for g in range(SSD_GROUPS):
        part = gz[:, g * half:(g + 1) * half]
        outs.append(part * lax.rsqrt(jnp.mean(part * part, axis=1, keepdims=True) + RMS_EPS))
    y_ref[...] = jnp.concatenate(outs, axis=1) * ng_ref[...]


def ssd_decode(xbc, z, dt_raw, state_conv, state_ssm, conv_w, conv_b, dt_bias, a_log, d_skip, norm_g, layer):
    db, cdim = xbc.shape
    d_ssd = N_HEADS_S * 64
    one = lambda width: pl.BlockSpec((None, 1, width), lambda b: (b, 0, 0))
    vec = lambda width: pl.BlockSpec((1, width), lambda b: (0, 0))
    ssm_shape = state_ssm.shape[2:]
    return pl.pallas_call(
        _ssd_decode_kernel, grid=(db,),
        in_specs=[one(cdim), one(d_ssd), one(128),
                  pl.BlockSpec((None, None, CONV_K - 1, cdim), lambda b: (layer, b, 0, 0)),
                  pl.BlockSpec((None, None) + ssm_shape, lambda b: (layer, b, 0, 0, 0)),
                  pl.BlockSpec((None, CONV_K, cdim), lambda b: (layer, 0, 0)),
                  vec(cdim), vec(128), vec(128), vec(128), vec(d_ssd)],
        out_specs=[one(d_ssd),
                   pl.BlockSpec((None,) + ssm_shape, lambda b: (b, 0, 0, 0)),
                   pl.BlockSpec((None, CONV_K - 1, cdim), lambda b: (b, 0, 0))],
        out_shape=[jax.ShapeDtypeStruct((db, 1, d_ssd), F32),
                   jax.ShapeDtypeStruct((db,) + ssm_shape, F32),
                   jax.ShapeDtypeStruct((db, CONV_K - 1, cdim), F32)],
        compiler_params=_cparams("parallel"), name="ssd_decode",
    )(xbc.reshape(db, 1, cdim), z.reshape(db, 1, d_ssd), dt_raw.reshape(db, 1, 128), state_conv, state_ssm,
      conv_w, conv_b[layer].reshape(1, -1), _pad_lanes(dt_bias[layer]), _pad_lanes(a_log[layer]),
      _pad_lanes(d_skip[layer]), norm_g[layer].reshape(1, -1))


def _mem_attn_prompt_kernel(q_ref, k_ref, v_ref, o_ref, kbf_ref, vbf_ref):
    @pl.when(pl.program_id(1) == 0)
    def _():
        kbf_ref[...] = k_ref[...].astype(BF16)
        vbf_ref[...] = v_ref[...].astype(BF16)

    dh = q_ref.shape[1] // N_HEADS_MEM
    dims_nt = (((1,), (1,)), ((), ()))
    for h in range(N_HEADS_MEM):
        sl = slice(h * dh, (h + 1) * dh)
        s = lax.dot_general(q_ref[:, sl], kbf_ref[:, sl], dims_nt, preferred_element_type=F32) * (dh ** -0.5)
        e = jnp.exp(s - jnp.max(s, axis=1, keepdims=True))
        p = e / jnp.sum(e, axis=1, keepdims=True)
        o_ref[:, sl] = jnp.dot(p.astype(BF16), vbf_ref[:, sl], preferred_element_type=F32).astype(o_ref.dtype)


def mem_attn_prompt(qm, kv, batch, seq, n_mem, tq=512):
    d = qm.shape[1]
    nq = seq // tq
    return pl.pallas_call(
        _mem_attn_prompt_kernel, grid=(batch, nq),
        in_specs=[pl.BlockSpec((tq, d), lambda b, i: (b * nq + i, 0)),
                  pl.BlockSpec((n_mem, d), lambda b, i: (b, 0)),
                  pl.BlockSpec((n_mem, d), lambda b, i: (b, 1))],
        out_specs=pl.BlockSpec((tq, d), lambda b, i: (b * nq + i, 0)),
        out_shape=jax.ShapeDtypeStruct((batch * seq, d), BF16),
        scratch_shapes=[pltpu.VMEM((n_mem, d), BF16), pltpu.VMEM((n_mem, d), BF16)],
        compiler_params=_cparams("parallel", "arbitrary"), name="mem_attn_prompt",
    )(qm, kv, kv)


def _mem_attn_decode_kernel(q_ref, k_ref, v_ref, o_ref):
    q = q_ref[...]
    s = jnp.sum(k_ref[...] * q[None], axis=-1, keepdims=True) * (q.shape[-1] ** -0.5)
    e = jnp.exp(s - jnp.max(s, axis=0, keepdims=True))
    p = e / jnp.sum(e, axis=0, keepdims=True)
    o_ref[...] = jnp.sum(p * v_ref[...], axis=0)


def mem_attn_decode(q, cache_mem_k, cache_mem_v, layer):
    db = q.shape[0]
    blk = cache_mem_k.shape[2:]
    tok = pl.BlockSpec((None,) + q.shape[1:], lambda b: (b, 0, 0))
    kv = pl.BlockSpec((None, None) + blk, lambda b: (layer, b, 0, 0, 0))
    return pl.pallas_call(
        _mem_attn_decode_kernel, grid=(db,), in_specs=[tok, kv, kv], out_specs=tok,
        out_shape=jax.ShapeDtypeStruct(q.shape, F32),
        compiler_params=_cparams("parallel"), name="mem_attn_decode",
    )(q, cache_mem_k, cache_mem_v)


def _router_kernel(x_ref, w_ref, idx_ref, gate_ref):
    logits = jnp.dot(x_ref[...], w_ref[...], preferred_element_type=F32, precision=lax.Precision.HIGHEST)
    n_exp = logits.shape[1]
    lane = lax.broadcasted_iota(jnp.int32, logits.shape, 1)
    m1 = jnp.max(logits, axis=1, keepdims=True)
    i1 = jnp.min(jnp.where(logits == m1, lane, n_exp), axis=1, keepdims=True)
    rest = jnp.where(lane == i1, -jnp.inf, logits)
    m2 = jnp.max(rest, axis=1, keepdims=True)
    i2 = jnp.min(jnp.where(rest == m2, lane, n_exp), axis=1, keepdims=True)
    e2 = jnp.exp(m2 - m1)
    idx_ref[:, 0:1] = i1
    idx_ref[:, 1:2] = i2
    gate_ref[:, 0:1] = 1.0 / (1.0 + e2)
    gate_ref[:, 1:2] = e2 / (1.0 + e2)


def router_top2(x, w_router, bm):
    m, d = x.shape
    n_exp = w_router.shape[-1]
    row = lambda w: pl.BlockSpec((bm, w), lambda i: (i, 0))
    return pl.pallas_call(
        _router_kernel, grid=(m // bm,),
        in_specs=[row(d), pl.BlockSpec((d, n_exp), lambda i: (0, 0))],
        out_specs=[row(TOP_K), row(TOP_K)],
        out_shape=[jax.ShapeDtypeStruct((m, TOP_K), jnp.int32), jax.ShapeDtypeStruct((m, TOP_K), F32)],
        compiler_params=_cparams("parallel"), name="router",
    )(x, w_router)


def _moe_plan(idx, n_exp, bm, n_tiles):
    e_flat = idx.reshape(-1)
    onehot = (e_flat[:, None] == jnp.arange(n_exp, dtype=jnp.int32)[None, :]).astype(jnp.int32)
    csum = jnp.cumsum(onehot, axis=0)
    counts = csum[-1]
    padded = ((counts + bm - 1) // bm) * bm
    gend = jnp.cumsum(padded)
    gstart = gend - padded
    dest = jnp.sum(onehot * (gstart[None, :] + csum - 1), axis=1).astype(jnp.int32)
    tile_start = jnp.arange(n_tiles, dtype=jnp.int32) * bm
    tile_expert = jnp.sum((tile_start[:, None] >= gend[None, :]).astype(jnp.int32), axis=1)
    n_valid = (gend[-1] // bm).astype(jnp.int32).reshape(1)
    return dest, jnp.minimum(tile_expert, n_exp - 1).astype(jnp.int32), n_valid


def _row_copy(src_ref, src_row, dst_ref, dst_row, sem):
    return pltpu.make_async_copy(src_ref.at[pl.ds(src_row, 1), :], dst_ref.at[pl.ds(dst_row, 1), :], sem)


def _moe_scatter_kernel(bm, dest_ref, x_ref, xs_in_ref, xs_ref, sem):
    del xs_in_ref
    base = pl.program_id(0) * bm * TOP_K

    def start(r, _):
        for k in range(TOP_K):
            _row_copy(x_ref, r, xs_ref, dest_ref[base + r * TOP_K + k], sem).start()
        return 0

    def wait(r, _):
        for k in range(TOP_K):
            _row_copy(x_ref, 0, xs_ref, 0, sem).wait()
        return 0

    lax.fori_loop(0, bm, start, 0)
    lax.fori_loop(0, bm, wait, 0)


def moe_scatter(x, dest, n_rows, bm):
    m, d = x.shape
    zeros = jnp.zeros((n_rows, d), x.dtype)
    return pl.pallas_call(
        functools.partial(_moe_scatter_kernel, bm),
        grid_spec=pltpu.PrefetchScalarGridSpec(
            num_scalar_prefetch=1, grid=(m // bm,),
            in_specs=[pl.BlockSpec((bm, d), lambda i, dest: (i, 0)), pl.BlockSpec(memory_space=pl.ANY)],
            out_specs=pl.BlockSpec(memory_space=pl.ANY),
            scratch_shapes=[pltpu.SemaphoreType.DMA(())]),
        out_shape=jax.ShapeDtypeStruct((n_rows, d), x.dtype),
        input_output_aliases={2: 0},
        compiler_params=_cparams("arbitrary"), name="moe_scatter",
    )(dest, x, zeros)


def _expert_changed(te_ref, i):
    return jnp.logical_or(i == 0, te_ref[i] != te_ref[jnp.maximum(i - 1, 0)])


def _gmm_up_kernel(te_ref, nv_ref, x_ref, wg_ref, wu_ref, o_ref, wgbf_ref, wubf_ref):
    i = pl.program_id(1)

    @pl.when(_expert_changed(te_ref, i))
    def _():
        wgbf_ref[...] = wg_ref[...].astype(BF16)
        wubf_ref[...] = wu_ref[...].astype(BF16)

    @pl.when(i < nv_ref[0])
    def _():
        x = x_ref[...].astype(BF16)
        g = jnp.dot(x, wgbf_ref[...], preferred_element_type=F32)
        u = jnp.dot(x, wubf_ref[...], preferred_element_type=F32)
        o_ref[...] = (g * jax.nn.sigmoid(g) * u).astype(o_ref.dtype)

    @pl.when(i >= nv_ref[0])
    def _():
        o_ref[...] = jnp.zeros_like(o_ref)


def gmm_up(xs, wg, wu, tile_expert, n_valid, bm, bn):
    rows, k = xs.shape
    f = wg.shape[-1]
    w_spec = pl.BlockSpec((None, None, k, bn), lambda j, i, te, nv: (0, te[i], 0, j))
    return pl.pallas_call(
        _gmm_up_kernel,
        grid_spec=pltpu.PrefetchScalarGridSpec(
            num_scalar_prefetch=2, grid=(f // bn, rows // bm),
            in_specs=[pl.BlockSpec((bm, k), lambda j, i, te, nv: (i, 0)), w_spec, w_spec],
            out_specs=pl.BlockSpec((bm, bn), lambda j, i, te, nv: (i, j)),
            scratch_shapes=[pltpu.VMEM((k, bn), BF16), pltpu.VMEM((k, bn), BF16)]),
        out_shape=jax.ShapeDtypeStruct((rows, f), BF16),
        compiler_params=_cparams("parallel", "arbitrary"), name="gmm_up",
    )(tile_expert, n_valid, xs, wg, wu)


def _gmm_down_kernel(te_ref, nv_ref, a_ref, w_ref, o_ref, wbf_ref):
    i = pl.program_id(1)

    @pl.when(_expert_changed(te_ref, i))
    def _():
        wbf_ref[...] = w_ref[...].astype(BF16)

    @pl.when(i < nv_ref[0])
    def _():
        o_ref[...] = jnp.dot(a_ref[...], wbf_ref[...], preferred_element_type=F32)

    @pl.when(i >= nv_ref[0])
    def _():
        o_ref[...] = jnp.zeros_like(o_ref)


def gmm_down(a, wd, tile_expert, n_valid, bm, bn):
    rows, f = a.shape
    d = wd.shape[-1]
    return pl.pallas_call(
        _gmm_down_kernel,
        grid_spec=pltpu.PrefetchScalarGridSpec(
            num_scalar_prefetch=2, grid=(d // bn, rows // bm),
            in_specs=[pl.BlockSpec((bm, f), lambda j, i, te, nv: (i, 0)),
                      pl.BlockSpec((None, None, f, bn), lambda j, i, te, nv: (0, te[i], 0, j))],
            out_specs=pl.BlockSpec((bm, bn), lambda j, i, te, nv: (i, j)),
            scratch_shapes=[pltpu.VMEM((f, bn), BF16)]),
        out_shape=jax.ShapeDtypeStruct((rows, d), F32),
        compiler_params=_cparams("parallel", "arbitrary"), name="gmm_down",
    )(tile_expert, n_valid, a, wd)


def _moe_combine_kernel(bm, alpha, dest_ref, x_ref, gate_ref, g_ref, b_ref, ys_ref, o_ref, obf_ref, buf_ref, sem):
    base = pl.program_id(0) * bm * TOP_K

    def start(r, _):
        for k in range(TOP_K):
            _row_copy(ys_ref, dest_ref[base + r * TOP_K + k], buf_ref.at[k], r, sem).start()
        return 0

    def wait(r, _):
        for k in range(TOP_K):
            _row_copy(ys_ref, 0, buf_ref.at[k], 0, sem).wait()
        return 0

    lax.fori_loop(0, bm, start, 0)
    lax.fori_loop(0, bm, wait, 0)
    f = gate_ref[:, 0:1] * buf_ref[0]
    for k in range(1, TOP_K):
        f = f + gate_ref[:, k:k + 1] * buf_ref[k]
    y = _ln_rows(alpha * x_ref[...] + f, g_ref[...], b_ref[...])
    o_ref[...] = y
    obf_ref[...] = y.astype(BF16)


def moe_combine(x, gates, ys, dest, g, b, alpha, bm):
    m, d = x.shape
    row = lambda w: pl.BlockSpec((bm, w), lambda i, dest: (i, 0))
    vec = pl.BlockSpec((1, d), lambda i, dest: (0, 0))
    return pl.pallas_call(
        functools.partial(_moe_combine_kernel, bm, alpha),
        grid_spec=pltpu.PrefetchScalarGridSpec(
            num_scalar_prefetch=1, grid=(m // bm,),
            in_specs=[row(d), row(TOP_K), vec, vec, pl.BlockSpec(memory_space=pl.ANY)],
            out_specs=[row(d), row(d)],
            scratch_shapes=[pltpu.VMEM((TOP_K, bm, d), F32), pltpu.SemaphoreType.DMA(())]),
        out_shape=[jax.ShapeDtypeStruct((m, d), F32), jax.ShapeDtypeStruct((m, d), BF16)],
        compiler_params=_cparams("arbitrary"), name="moe_combine",
    )(dest, x, gates, g.reshape(1, d), b.reshape(1, d), ys)


def moe_ffn_ln(x, w_router, wg, wu, wd, g, b, alpha, moe_layer):
    m, d = x.shape
    n_exp = w_router.shape[-1]
    bm_tok = _row_block(m, 320)
    bm = 512
    n_tiles = -(-(m * TOP_K + n_exp * (bm - 1)) // bm)
    idx, gates = router_top2(x, w_router[moe_layer], bm_tok)
    dest, tile_expert, n_valid = _moe_plan(idx, n_exp, bm, n_tiles)
    xs = moe_scatter(x, dest, n_tiles * bm, bm_tok)
    a = gmm_up(xs, wg[moe_layer:moe_layer + 1], wu[moe_layer:moe_layer + 1], tile_expert, n_valid, bm, 512)
    ys = gmm_down(a, wd[moe_layer:moe_layer + 1], tile_expert, n_valid, bm, 256)
    return moe_combine(x, gates, ys, dest, g, b, alpha, bm_tok)


def kernel(x_prompt, x_sample, mem_prompt, cache_k, cache_v, cache_mem_k, cache_mem_v, state_ssm, state_conv, page_table, ln_in_g, ln_in_b, w_in, conv_w, conv_b, dt_bias, a_log, d_skip, ssd_norm_g, lam_params, subln_g, w_out, w_mem_q, w_mem_kv, w_mem_o, ln_g, ln_b, w_ff_gate, w_ff_up, w_ff_down, w_router, w_exp_gate, w_exp_up, w_exp_down):
    batch, seq, d = x_prompt.shape
    db = x_sample.shape[0]
    assert x_sample.shape[1] == 1
    depth = w_in.shape[0]
    n_mem = mem_prompt.shape[1]
    mp = batch * seq
    m = mp + db
    alpha = (2 * depth) ** 0.25
    d_attn = N_HEADS_A * 128
    d_ssd = N_HEADS_S * 64
    cdim = d_ssd + 2 * SSD_GROUPS * SSD_STATE
    n_main = 3 * d_attn + d_ssd + cdim
    bm = _row_block(m, 1040)
    bm_small = _row_block(m, 416)

    x_all = jnp.concatenate([x_prompt.reshape(mp, d), x_sample.reshape(db, d)], axis=0)
    x, x_bf = layer_norm_in(x_all, ln_in_g, ln_in_b)
    mem2d = mem_prompt.reshape(batch * n_mem, d)

    outs = {k: [] for k in ("kp", "vp", "sp", "cp", "mkp", "mvp", "ks", "vs", "ss", "cs")}
    for l in range(depth):
        lam_init = 0.8 - 0.6 * math.exp(-0.3 * l)
        u = matmul(x_bf, w_in, lead=(l,), ncols=n_main, bm=bm, bn=512, name="mm_in")
        w_dt = jnp.pad(w_in[l][:, n_main:], ((0, 0), (0, 128 - N_HEADS_S)))
        dt_raw = matmul(x_bf, w_dt, bm=bm, bn=128, name="mm_dt")
        o_a = attn_prompt(u, lam_params, subln_g, l, batch, seq, lam_init)
        y_s, ssm_p, conv_p = ssd_prompt(u, dt_raw, conv_w, conv_b, dt_bias, a_log, d_skip, ssd_norm_g, l, batch, seq)
        us = u[mp:]
        heads = lambda a: a.reshape(db, N_HEADS_A, 128)
        o_a_s = attn_decode(heads(us[:, :d_attn]), heads(us[:, d_attn:2 * d_attn]), heads(us[:, 2 * d_attn:3 * d_attn]),
                            cache_k, cache_v, page_table, lam_params, subln_g, l, lam_init)
        y_s_s, ssm_s, conv_s = ssd_decode(us[:, n_main - cdim:n_main], us[:, 3 * d_attn:3 * d_attn + d_ssd], dt_raw[mp:],
                                          state_conv, state_ssm, conv_w, conv_b, dt_bias, a_log, d_skip, ssd_norm_g, l)
        mix = jnp.concatenate([jnp.concatenate([o_a, y_s], axis=1),
                               jnp.concatenate([o_a_s.reshape(db, d_attn), y_s_s.reshape(db, d_ssd)], axis=1).astype(BF16)],
                              axis=0)
        a = matmul(mix, w_out, lead=(l,), bm=bm, bn=512, name="mm_out")
        x, x_bf = add_layer_norm(x, a, ln_g[l, 0], ln_b[l, 0], alpha)
        qm = matmul(x_bf, w_mem_q, lead=(l,), bm=bm, bn=512, out_dtype=BF16, name="mm_mem_q")
        kv = matmul(mem2d, w_mem_kv, lead=(l,), bm=_row_block(batch * n_mem, 512), bn=512, name="mm_mem_kv")
        c_p = mem_attn_prompt(qm, kv, batch, seq, n_mem)
        c_s = mem_attn_decode(qm[mp:].astype(F32).reshape(db, N_HEADS_MEM, d // N_HEADS_MEM), cache_mem_k, cache_mem_v, l)
        c = jnp.concatenate([c_p, c_s.reshape(db, d).astype(BF16)], axis=0)
        a = matmul(c, w_mem_o, lead=(l,), bm=bm, bn=512, name="mm_mem_o")
        x, x_bf = add_layer_norm(x, a, ln_g[l, 1], ln_b[l, 1], alpha)
        if l % 2 == 0:
            act = swiglu_up(x_bf, w_ff_gate, w_ff_up, lead=(l // 2,), bm=bm, bn=512)
            a = matmul(act, w_ff_down, lead=(l // 2,), bm=bm_small, bn=512, name="mm_ff_down")
            x, x_bf = add_layer_norm(x, a, ln_g[l, 2], ln_b[l, 2], alpha)
        else:
            x, x_bf = moe_ffn_ln(x, w_router, w_exp_gate, w_exp_up, w_exp_down, ln_g[l, 2], ln_b[l, 2], alpha, l // 2)
        outs["kp"].append(u[:mp, d_attn:2 * d_attn].reshape(batch, seq, N_HEADS_A, 128))
        outs["vp"].append(u[:mp, 2 * d_attn:3 * d_attn].reshape(batch, seq, N_HEADS_A, 128))
        outs["sp"].append(ssm_p.reshape(batch, N_HEADS_S, 64, SSD_STATE))
        outs["cp"].append(conv_p[:, 8 - (CONV_K - 1):, :])
        outs["mkp"].append(kv[:, :d].reshape(batch, n_mem, N_HEADS_MEM, d // N_HEADS_MEM))
        outs["mvp"].append(kv[:, d:].reshape(batch, n_mem, N_HEADS_MEM, d // N_HEADS_MEM))
        outs["ks"].append(us[:, d_attn:2 * d_attn].reshape(db, 1, N_HEADS_A, 128))
        outs["vs"].append(us[:, 2 * d_attn:3 * d_attn].reshape(db, 1, N_HEADS_A, 128))
        outs["ss"].append(ssm_s)
        outs["cs"].append(conv_s)
    st = {k: jnp.stack(v) for k, v in outs.items()}
    return (x[:mp].reshape(batch, seq, d), x[mp:].reshape(db, 1, d),
            st["kp"], st["vp"], st["sp"], st["cp"], st["mkp"], st["mvp"], st["ks"], st["vs"], st["ss"], st["cs"])
```

```python
import functools
import math

import jax
import jax.numpy as jnp
from jax import lax
from jax.experimental import pallas as pl
from jax.experimental.pallas import tpu as pltpu

F32 = jnp.float32
BF16 = jnp.bfloat16

LN_EPS = 1e-5
RMS_EPS = 1e-5
N_HEADS_A = 8
QK_DIM = 64
N_HEADS_S = 16
SSD_GROUPS = 2
SSD_STATE = 128
SSD_CHUNK = 128
CONV_K = 4
N_HEADS_MEM = 4
TOP_K = 2
SSD_DECODE_SEQS = 4
VMEM_LIMIT = 56 * 1024 * 1024


def _cparams(*sem):
    return pltpu.CompilerParams(dimension_semantics=sem, vmem_limit_bytes=VMEM_LIMIT)


def _row_block(m, target):
    best = None
    for d in range(16, min(m, target) + 1, 16):
        if m % d == 0:
            best = d
    assert best is not None, (m, target)
    return best


def _ln_rows(x, g, b):
    mu = jnp.mean(x, axis=-1, keepdims=True)
    xc = x - mu
    var = jnp.mean(xc * xc, axis=-1, keepdims=True)
    return xc * lax.rsqrt(var + LN_EPS) * g + b


def _ln_kernel(x_ref, g_ref, b_ref, o_ref, obf_ref):
    y = _ln_rows(x_ref[...], g_ref[...], b_ref[...])
    o_ref[...] = y
    obf_ref[...] = y.astype(BF16)


def layer_norm_in(x, g, b):
    m, d = x.shape
    bm = _row_block(m, 512)
    row = pl.BlockSpec((bm, d), lambda i: (i, 0))
    vec = pl.BlockSpec((1, d), lambda i: (0, 0))
    return pl.pallas_call(
        _ln_kernel, grid=(m // bm,), in_specs=[row, vec, vec], out_specs=[row, row],
        out_shape=[jax.ShapeDtypeStruct((m, d), F32), jax.ShapeDtypeStruct((m, d), BF16)],
        compiler_params=_cparams("parallel"), name="ln_in",
    )(x, g.reshape(1, d), b.reshape(1, d))


def _add_ln_kernel(alpha, x_ref, a_ref, g_ref, b_ref, o_ref, obf_ref):
    y = _ln_rows(alpha * x_ref[...] + a_ref[...], g_ref[...], b_ref[...])
    o_ref[...] = y
    obf_ref[...] = y.astype(BF16)


def add_layer_norm(x, a, g, b, alpha):
    m, d = x.shape
    bm = _row_block(m, 512)
    row = pl.BlockSpec((bm, d), lambda i: (i, 0))
    vec = pl.BlockSpec((1, d), lambda i: (0, 0))
    return pl.pallas_call(
        functools.partial(_add_ln_kernel, alpha), grid=(m // bm,),
        in_specs=[row, row, vec, vec], out_specs=[row, row],
        out_shape=[jax.ShapeDtypeStruct((m, d), F32), jax.ShapeDtypeStruct((m, d), BF16)],
        compiler_params=_cparams("parallel"), name="add_ln",
    )(x, a, g.reshape(1, d), b.reshape(1, d))


def _mm_kernel(x_ref, w_ref, o_ref, wbf_ref):
    @pl.when(pl.program_id(1) == 0)
    def _():
        wbf_ref[...] = w_ref[...].astype(BF16)

    o_ref[...] = jnp.dot(x_ref[...].astype(BF16), wbf_ref[...],
                         preferred_element_type=F32).astype(o_ref.dtype)


def matmul(x, w, *, lead=(), col0=0, ncols=None, bm, bn, out_dtype=F32, name="mm"):
    m, k = x.shape
    n = w.shape[-1] - col0 if ncols is None else ncols
    assert w.shape[-2] == k and m % bm == 0 and n % bn == 0 and col0 % bn == 0
    nl = len(lead)
    cb0 = col0 // bn
    w_spec = pl.BlockSpec((None,) * nl + (k, bn), lambda j, i: tuple(lead) + (0, cb0 + j))
    return pl.pallas_call(
        _mm_kernel, grid=(n // bn, m // bm),
        in_specs=[pl.BlockSpec((bm, k), lambda j, i: (i, 0)), w_spec],
        out_specs=pl.BlockSpec((bm, bn), lambda j, i: (i, j)),
        out_shape=jax.ShapeDtypeStruct((m, n), out_dtype),
        scratch_shapes=[pltpu.VMEM((k, bn), BF16)],
        compiler_params=_cparams("parallel", "arbitrary"), name=name,
    )(x, w)


def _swiglu_up_kernel(x_ref, wg_ref, wu_ref, o_ref, wgbf_ref, wubf_ref):
    @pl.when(pl.program_id(1) == 0)
    def _():
        wgbf_ref[...] = wg_ref[...].astype(BF16)
        wubf_ref[...] = wu_ref[...].astype(BF16)

    x = x_ref[...]
    g = jnp.dot(x, wgbf_ref[...], preferred_element_type=F32)
    u = jnp.dot(x, wubf_ref[...], preferred_element_type=F32)
    o_ref[...] = (g * jax.nn.sigmoid(g) * u).astype(o_ref.dtype)


def swiglu_up(x, wg, wu, *, lead, bm, bn):
    m, k = x.shape
    n = wg.shape[-1]
    assert m % bm == 0 and n % bn == 0
    nl = len(lead)
    w_spec = pl.BlockSpec((None,) * nl + (k, bn), lambda j, i: tuple(lead) + (0, j))
    return pl.pallas_call(
        _swiglu_up_kernel, grid=(n // bn, m // bm),
        in_specs=[pl.BlockSpec((bm, k), lambda j, i: (i, 0)), w_spec, w_spec],
        out_specs=pl.BlockSpec((bm, bn), lambda j, i: (i, j)),
        out_shape=jax.ShapeDtypeStruct((m, n), BF16),
        scratch_shapes=[pltpu.VMEM((k, bn), BF16), pltpu.VMEM((k, bn), BF16)],
        compiler_params=_cparams("parallel", "arbitrary"), name="swiglu_up",
    )(x, wg, wu)


def _lambda_value(lp, lam_init):
    t1 = jnp.sum(lp[0:1, :] * lp[1:2, :], axis=1, keepdims=True)
    t2 = jnp.sum(lp[2:3, :] * lp[3:4, :], axis=1, keepdims=True)
    return jnp.exp(t1) - jnp.exp(t2) + lam_init


LOG2E = 1.4426950408889634


def _attn_prompt_kernel(tq, lam_init, q_ref, k_ref, v_ref, lp_ref, g_ref, o_ref, kbf_ref, vt_ref):
    h = pl.program_id(1)
    qi = pl.program_id(2)

    @pl.when(qi == 0)
    def _():
        kbf_ref[...] = k_ref[...].astype(BF16)
        for c in range(vt_ref.shape[0]):
            vt_ref[c] = v_ref[c * tq:(c + 1) * tq, :].T.astype(BF16)

    slope = jnp.exp2(-(h + 1).astype(F32) * jnp.ones((1, 1), F32)) * LOG2E
    lam = _lambda_value(lp_ref[...], lam_init)
    qt = (q_ref[...] * (QK_DIM ** -0.5 * LOG2E)).T
    sub = lax.broadcasted_iota(jnp.int32, qt.shape, 0)
    qts = (jnp.where(sub < QK_DIM, qt, 0.0).astype(BF16), jnp.where(sub >= QK_DIM, qt, 0.0).astype(BF16))
    krow = lax.broadcasted_iota(jnp.int32, (tq, tq), 0)
    qcol = lax.broadcasted_iota(jnp.int32, (tq, tq), 1)
    base = -slope * (qcol - krow).astype(F32)

    def chunk(kj, carry, masked):
        kc = kbf_ref[pl.ds(pl.multiple_of(kj * tq, tq), tq), :]
        vtc = vt_ref[kj]
        off = -slope * ((qi - kj) * tq).astype(F32)
        out = []
        for qm, (m, l, acc) in zip(qts, carry):
            t = jnp.dot(kc, qm, preferred_element_type=F32) + base
            if masked:
                t = jnp.where(krow <= qcol, t, -jnp.inf)
            m_new = jnp.maximum(m, jnp.max(t, axis=0, keepdims=True) + off)
            p = jnp.exp2(t - (m_new - off))
            alpha = jnp.exp2(m - m_new)
            l_new = alpha * l + jnp.sum(p, axis=0, keepdims=True)
            acc_new = alpha * acc + jnp.dot(vtc, p.astype(BF16), preferred_element_type=F32)
            out.append((m_new, l_new, acc_new))
        return tuple(out)

    init_one = (jnp.full((1, tq), -1e30, F32), jnp.zeros((1, tq), F32), jnp.zeros((128, tq), F32))
    carry = lax.fori_loop(0, qi, lambda kj, c: chunk(kj, c, False), (init_one, init_one))
    (_, l1, a1), (_, l2, a2) = chunk(qi, carry, True)
    o = (a1 / l1 - lam * (a2 / l2)).T
    o = o * lax.rsqrt(jnp.mean(o * o, axis=1, keepdims=True) + RMS_EPS)
    o_ref[...] = (o * g_ref[...] * (1.0 - lam_init)).astype(o_ref.dtype)


def attn_prompt(u, lam_params, subln_g, layer, batch, seq, lam_init, tq=512):
    nq = seq // tq
    lp_spec = pl.BlockSpec((None, 4, QK_DIM), lambda b, h, i: (layer, 0, 0))
    g_spec = pl.BlockSpec((None, 1, 128), lambda b, h, i: (layer, 0, 0))
    return pl.pallas_call(
        functools.partial(_attn_prompt_kernel, tq, lam_init),
        grid=(batch, N_HEADS_A, nq),
        in_specs=[pl.BlockSpec((tq, 128), lambda b, h, i: (b * nq + i, h)),
                  pl.BlockSpec((seq, 128), lambda b, h, i: (b, N_HEADS_A + h)),
                  pl.BlockSpec((seq, 128), lambda b, h, i: (b, 2 * N_HEADS_A + h)),
                  lp_spec, g_spec],
        out_specs=pl.BlockSpec((tq, 128), lambda b, h, i: (b * nq + i, h)),
        out_shape=jax.ShapeDtypeStruct((batch * seq, N_HEADS_A * 128), BF16),
        scratch_shapes=[pltpu.VMEM((seq, 128), BF16), pltpu.VMEM((nq, 128, tq), BF16)],
        compiler_params=_cparams("parallel", "parallel", "arbitrary"), name="attn_prompt",
    )(u, u, u, lam_params, subln_g.reshape(-1, 1, 128))


def _softplus(x):
    return jnp.maximum(x, 0.0) + jnp.log1p(jnp.exp(-jnp.abs(x)))


def _silu(x):
    return x * jax.nn.sigmoid(x)


def _ssd_prompt_kernel(xs_ref, bc_ref, z_ref, dt_ref, cw_ref, cb_ref, dtb_ref, alog_ref, dsk_ref, ng_ref,
                       y_ref, st_ref, conv_ref, xp_ref, h_ref):
    c = pl.program_id(1)
    nc = pl.num_programs(1)
    q = SSD_CHUNK
    d_ssd = N_HEADS_S * 64

    @pl.when(c == 0)
    def _():
        xp_ref[0:8, :] = jnp.zeros((8, xp_ref.shape[1]), F32)
        h_ref[...] = jnp.zeros_like(h_ref)

    xp_ref[8:8 + q, 0:d_ssd] = xs_ref[...]
    xp_ref[8:8 + q, d_ssd:] = bc_ref[...]
    cw = cw_ref[...]
    xc = cb_ref[...] + cw[3:4, :] * xp_ref[8:8 + q, :]
    for j in range(1, CONV_K):
        xc = xc + cw[3 - j:4 - j, :] * xp_ref[8 - j:8 - j + q, :]
    xp_ref[0:8, :] = xp_ref[q:q + 8, :]
    xc = _silu(xc)
    xs = xc[:, :d_ssd]

    dt = _softplus(dt_ref[...] + dtb_ref[...])
    a_neg = -jnp.exp(alog_ref[...])
    da = dt * a_neg
    ri = lax.broadcasted_iota(jnp.int32, (q, q), 0)
    ci = lax.broadcasted_iota(jnp.int32, (q, q), 1)
    causal = ci <= ri
    tril = jnp.where(causal, 1.0, 0.0).astype(F32)
    a_cs = jnp.dot(tril, da, preferred_element_type=F32, precision=lax.Precision.HIGHEST)
    a_cs_t = a_cs.T
    a_last = a_cs[q - 1:q, :]
    e_cs = jnp.exp(a_cs)
    e_end = jnp.exp(a_last - a_cs)
    e_last = jnp.exp(a_last)
    lane = lax.broadcasted_iota(jnp.int32, (q, 128), 1)
    lo = lane < 64
    rsel = lax.broadcasted_iota(jnp.int32, (128, SSD_STATE), 0) < 64
    dims_nt = (((1,), (1,)), ((), ()))
    dims_tn = (((0,), (0,)), ((), ()))
    hpg = N_HEADS_S // SSD_GROUPS

    ys = []
    for g in range(SSD_GROUPS):
        bm_g = xc[:, d_ssd + g * SSD_STATE:d_ssd + (g + 1) * SSD_STATE].astype(BF16)
        cm_g = xc[:, d_ssd + (SSD_GROUPS + g) * SSD_STATE:d_ssd + (SSD_GROUPS + g + 1) * SSD_STATE].astype(BF16)
        cb = lax.dot_general(cm_g, bm_g, dims_nt, preferred_element_type=F32)
        for pr in range(hpg // 2):
            h0 = g * hpg + 2 * pr
            x_pair = xs[:, h0 * 64:h0 * 64 + 128]
            dt_pair = jnp.where(lo, dt[:, h0:h0 + 1], dt[:, h0 + 1:h0 + 2])
            xdt = x_pair * dt_pair
            y_pair = jnp.zeros((q, 128), F32)
            for k, keep in ((0, lo), (1, jnp.logical_not(lo))):
                hh = h0 + k
                seg = a_cs[:, hh:hh + 1] - a_cs_t[hh:hh + 1, :]
                decay = jnp.exp(jnp.where(causal, seg, -jnp.inf))
                mat = (cb * decay).astype(BF16)
                y_pair = y_pair + jnp.dot(mat, jnp.where(keep, xdt, 0.0).astype(BF16), preferred_element_type=F32)
            end_pair = jnp.where(lo, e_end[:, h0:h0 + 1], e_end[:, h0 + 1:h0 + 2])
            cs_pair = jnp.where(lo, e_cs[:, h0:h0 + 1], e_cs[:, h0 + 1:h0 + 2])
            h_prev = h_ref[h0 * 64:h0 * 64 + 128, :]
            y_off = lax.dot_general(cm_g, h_prev.astype(BF16), dims_nt, preferred_element_type=F32) * cs_pair
            st = lax.dot_general((xdt * end_pair).astype(BF16), bm_g, dims_tn, preferred_element_type=F32)
            dec = jnp.where(rsel, e_last[:, h0:h0 + 1], e_last[:, h0 + 1:h0 + 2])
            h_ref[h0 * 64:h0 * 64 + 128, :] = dec * h_prev + st
            dsk_pair = jnp.where(lo[0:1, :], dsk_ref[:, h0:h0 + 1], dsk_ref[:, h0 + 1:h0 + 2])
            ys.append(y_pair + y_off + dsk_pair * x_pair)
    y = jnp.concatenate(ys, axis=1)
    gz = y * _silu(z_ref[...])
    half = d_ssd // SSD_GROUPS
    outs = []
    for g in range(SSD_GROUPS):
        part = gz[:, g * half:(g + 1) * half]
        outs.append(part * lax.rsqrt(jnp.mean(part * part, axis=1, keepdims=True) + RMS_EPS))
    y_ref[...] = (jnp.concatenate(outs, axis=1) * ng_ref[...]).astype(y_ref.dtype)

    @pl.when(c == nc - 1)
    def _():
        st_ref[...] = h_ref[...]
        conv_ref[:, 0:d_ssd] = xs_ref[q - 8:q, :]
        conv_ref[:, d_ssd:] = bc_ref[q - 8:q, :]


def _pad_lanes(v):
    return jnp.pad(v.reshape(1, -1), ((0, 0), (0, 128 - v.shape[-1])))


def ssd_prompt(u, dt_raw, conv_w, conv_b, dt_bias, a_log, d_skip, norm_g, layer, batch, seq):
    q = SSD_CHUNK
    nc = seq // q
    d_ssd = N_HEADS_S * 64
    d_bc = 2 * SSD_GROUPS * SSD_STATE
    cdim = d_ssd + d_bc
    d_attn = N_HEADS_A * 128
    z0, x0, bc0 = 3 * d_attn, 3 * d_attn + d_ssd, 3 * d_attn + 2 * d_ssd
    assert z0 % d_ssd == 0 and x0 % d_ssd == 0 and bc0 % d_bc == 0
    row = lambda b, c: (b * nc + c, 0)
    vec = lambda width: pl.BlockSpec((1, width), lambda b, c: (0, 0))
    return pl.pallas_call(
        _ssd_prompt_kernel, grid=(batch, nc),
        in_specs=[pl.BlockSpec((q, d_ssd), lambda b, c: (b * nc + c, x0 // d_ssd)),
                  pl.BlockSpec((q, d_bc), lambda b, c: (b * nc + c, bc0 // d_bc)),
                  pl.BlockSpec((q, d_ssd), lambda b, c: (b * nc + c, z0 // d_ssd)),
                  pl.BlockSpec((q, 128), row),
                  pl.BlockSpec((None, CONV_K, cdim), lambda b, c: (layer, 0, 0)),
                  vec(cdim), vec(128), vec(128), vec(128), vec(d_ssd)],
        out_specs=[pl.BlockSpec((q, d_ssd), row),
                   pl.BlockSpec((None, N_HEADS_S * 64, SSD_STATE), lambda b, c: (b, 0, 0)),
                   pl.BlockSpec((None, 8, cdim), lambda b, c: (b, 0, 0))],
        out_shape=[jax.ShapeDtypeStruct((batch * seq, d_ssd), BF16),
                   jax.ShapeDtypeStruct((batch, N_HEADS_S * 64, SSD_STATE), F32),
                   jax.ShapeDtypeStruct((batch, 8, cdim), F32)],
        scratch_shapes=[pltpu.VMEM((q + 8, cdim), F32), pltpu.VMEM((N_HEADS_S * 64, SSD_STATE), F32)],
        compiler_params=_cparams("parallel", "arbitrary"), name="ssd_prompt",
    )(u, u, u, dt_raw, conv_w, conv_b[layer].reshape(1, -1), _pad_lanes(dt_bias[layer]),
      _pad_lanes(a_log[layer]), _pad_lanes(d_skip[layer]), norm_g[layer].reshape(1, -1))


def _attn_decode_kernel(n_pages, page, lam_init, pt_ref, q_ref, kn_ref, vn_ref, lp_ref, g_ref, *refs):
    k_refs, v_refs = refs[:n_pages], refs[n_pages:2 * n_pages]
    o_ref, s_ref = refs[2 * n_pages], refs[2 * n_pages + 1]
    lam = _lambda_value(lp_ref[...], lam_init)
    q = q_ref[...] * (QK_DIM ** -0.5 * LOG2E)
    lo = lax.broadcasted_iota(jnp.int32, q.shape, 1) < QK_DIM
    slope = jnp.exp2(-(lax.broadcasted_iota(jnp.int32, q.shape, 0) + 1).astype(F32)) * LOG2E
    tok_bias = slope[None] * lax.broadcasted_iota(jnp.int32, (page, 1, 1), 0).astype(F32)
    past = n_pages * page

    ri = lax.broadcasted_iota(jnp.int32, (128, 128), 0) < QK_DIM
    ci = lax.broadcasted_iota(jnp.int32, (128, 128), 1) < QK_DIM
    half_sum = jnp.where(ri == ci, 1.0, 0.0).astype(BF16)

    def packed_scores(k):
        prod = (k * q).reshape(-1, 128).astype(BF16)
        return jnp.dot(prod, half_sum, preferred_element_type=F32).reshape(k.shape)

    s_new = packed_scores(kn_ref[...])
    m = s_new
    page_bias = [slope * float(past - p * page) for p in range(n_pages)]
    for p in range(n_pages):
        s = packed_scores(k_refs[p][...]) + tok_bias
        s_ref[p * page:(p + 1) * page] = s
        m = jnp.maximum(m, jnp.max(s, axis=0) - page_bias[p])
    e_new = jnp.exp2(s_new - m)
    l = e_new
    for p in range(n_pages):
        e = jnp.exp2(s_ref[p * page:(p + 1) * page] - (m + page_bias[p])[None])
        s_ref[p * page:(p + 1) * page] = e
        l = l + jnp.sum(e, axis=0)
    r = 1.0 / l
    coef = jnp.where(lo, r, -lam * r)
    w_new = e_new * coef
    acc = (w_new + pltpu.roll(w_new, QK_DIM, 1)) * vn_ref[...]
    for p in range(n_pages):
        w = s_ref[p * page:(p + 1) * page] * coef[None]
        w = w + pltpu.roll(w, QK_DIM, 2)
        acc = acc + jnp.sum(w * v_refs[p][...], axis=0)
    o = acc * lax.rsqrt(jnp.mean(acc * acc, axis=-1, keepdims=True) + RMS_EPS)
    o_ref[...] = o * g_ref[...] * (1.0 - lam_init)


def attn_decode(q, k_new, v_new, cache_k, cache_v, page_table, lam_params, subln_g, layer, lam_init):
    db, n_pages = page_table.shape
    page = cache_k.shape[2]

    def kv_spec(j):
        return pl.BlockSpec((None, None, page, N_HEADS_A, 128), lambda b, pt: (layer, pt[b, j], 0, 0, 0))

    tok = pl.BlockSpec((None, N_HEADS_A, 128), lambda b, pt: (b, 0, 0))
    in_specs = [tok, tok, tok,
                pl.BlockSpec((None, 4, QK_DIM), lambda b, pt: (layer, 0, 0)),
                pl.BlockSpec((None, 1, 128), lambda b, pt: (layer, 0, 0))]
    in_specs += [kv_spec(j) for j in range(n_pages)] * 2
    return pl.pallas_call(
        functools.partial(_attn_decode_kernel, n_pages, page, lam_init),
        grid_spec=pltpu.PrefetchScalarGridSpec(
            num_scalar_prefetch=1, grid=(db,), in_specs=in_specs, out_specs=tok,
            scratch_shapes=[pltpu.VMEM((n_pages * page, N_HEADS_A, 128), F32)]),
        out_shape=jax.ShapeDtypeStruct((db, N_HEADS_A, 128), F32),
        compiler_params=_cparams("parallel"), name="attn_decode",
    )(page_table, q, k_new, v_new, lam_params, subln_g.reshape(-1, 1, 128), *([cache_k] * n_pages),
      *([cache_v] * n_pages))


def _ssd_decode_kernel(xbc_ref, z_ref, dt_ref, cst_ref, ssm_ref, cw_ref, cb_ref, dtb_ref, alog_ref, dsk_ref,
                       ng_ref, y_ref, ssm_out_ref, conv_out_ref):
    for s in range(xbc_ref.shape[0]):
        _ssd_decode_one(xbc_ref.at[s], z_ref.at[s], dt_ref.at[s], cst_ref.at[s], ssm_ref.at[s], cw_ref, cb_ref, dtb_ref,
                        alog_ref, dsk_ref, ng_ref, y_ref.at[s], ssm_out_ref.at[s], conv_out_ref.at[s])


def _ssd_decode_one(xbc_ref, z_ref, dt_ref, cst_ref, ssm_ref, cw_ref, cb_ref, dtb_ref, alog_ref, dsk_ref,
                    ng_ref, y_ref, ssm_out_ref, conv_out_ref):
    d_ssd = N_HEADS_S * 64
    hpg = N_HEADS_S // SSD_GROUPS
    xnew = xbc_ref[...]
    cst = cst_ref[...]
    cw = cw_ref[...]
    xc = cb_ref[...] + cw[3:4, :] * xnew
    for j in range(CONV_K - 1):
        xc = xc + cw[j:j + 1, :] * cst[j:j + 1, :]
    xc = _silu(xc)
    conv_out_ref[0:2, :] = cst[1:3, :]
    conv_out_ref[2:3, :] = xnew

    dt = _softplus(dt_ref[...] + dtb_ref[...])
    dec = jnp.exp(dt * (-jnp.exp(alog_ref[...])))
    eye = lax.broadcasted_iota(jnp.int32, (128, 128), 0) == lax.broadcasted_iota(jnp.int32, (128, 128), 1)
    lo = lax.broadcasted_iota(jnp.int32, (1, 128), 1) < 64
    dims_nt = (((1,), (1,)), ((), ()))
    ys = []
    for pr in range(N_HEADS_S // 2):
        h0 = 2 * pr
        g = h0 // hpg
        x_pair = xc[:, h0 * 64:h0 * 64 + 128]
        b_g = xc[:, d_ssd + g * SSD_STATE:d_ssd + (g + 1) * SSD_STATE]
        c_g = xc[:, d_ssd + (SSD_GROUPS + g) * SSD_STATE:d_ssd + (SSD_GROUPS + g + 1) * SSD_STATE]
        dt_pair = jnp.where(lo, dt[:, h0:h0 + 1], dt[:, h0 + 1:h0 + 2])
        xdt_row = x_pair * dt_pair
        xdt_diag = jnp.where(eye, jnp.broadcast_to(xdt_row, (128, 128)), 0.0).astype(BF16)
        upd = jnp.dot(xdt_diag, jnp.broadcast_to(b_g, (128, SSD_STATE)).astype(BF16), preferred_element_type=F32)
        h_prev = ssm_ref[h0:h0 + 2]
        h_new = jnp.concatenate([jnp.broadcast_to(dec[:, h0 + k:h0 + k + 1], (64, SSD_STATE)) * h_prev[k]
                                 for k in range(2)], axis=0) + upd
        ssm_out_ref[h0:h0 + 2] = h_new.reshape(2, 64, SSD_STATE)
        y_rows = lax.dot_general(jnp.broadcast_to(c_g, (8, SSD_STATE)).astype(BF16), h_new.astype(BF16), dims_nt,
                                 preferred_element_type=F32)
        dsk_pair = jnp.where(lo, dsk_ref[:, h0:h0 + 1], dsk_ref[:, h0 + 1:h0 + 2])
        ys.append(y_rows[0:1, :] + dsk_pair * x_pair)
    y = jnp.concatenate(ys, axis=1)
    gz = y * _silu(z_ref[...])
    half = d_ssd // SSD_GROUPS
    outs = []
    for g in range(SSD_GROUPS):
        part = gz[:, g * half:(g + 1) * half]
        outs.append(part * lax.rsqrt(jnp.mean(part * part, axis=1, keepdims=True) + RMS_EPS))
    y_ref[...] = jnp.concatenate(outs, axis=1) * ng_ref[...]


def ssd_decode(xbc, z, dt_raw, state_conv, state_ssm, conv_w, conv_b, dt_bias, a_log, d_skip, norm_g, layer):
    db, cdim = xbc.shape
    d_ssd = N_HEADS_S * 64
    bb = SSD_DECODE_SEQS
    assert db % bb == 0
    one = lambda width: pl.BlockSpec((bb, 1, width), lambda b: (b, 0, 0))
    vec = lambda width: pl.BlockSpec((1, width), lambda b: (0, 0))
    ssm_shape = state_ssm.shape[2:]
    return pl.pallas_call(
        _ssd_decode_kernel, grid=(db // bb,),
        in_specs=[one(cdim), one(d_ssd), one(128),
                  pl.BlockSpec((None, bb, CONV_K - 1, cdim), lambda b: (layer, b, 0, 0)),
                  pl.BlockSpec((None, bb) + ssm_shape, lambda b: (layer, b, 0, 0, 0)),
                  pl.BlockSpec((None, CONV_K, cdim), lambda b: (layer, 0, 0)),
                  vec(cdim), vec(128), vec(128), vec(128), vec(d_ssd)],
        out_specs=[one(d_ssd),
                   pl.BlockSpec((bb,) + ssm_shape, lambda b: (b, 0, 0, 0)),
                   pl.BlockSpec((bb, CONV_K - 1, cdim), lambda b: (b, 0, 0))],
        out_shape=[jax.ShapeDtypeStruct((db, 1, d_ssd), F32),
                   jax.ShapeDtypeStruct((db,) + ssm_shape, F32),
                   jax.ShapeDtypeStruct((db, CONV_K - 1, cdim), F32)],
        compiler_params=_cparams("parallel"), name="ssd_decode",
    )(xbc.reshape(db, 1, cdim), z.reshape(db, 1, d_ssd), dt_raw.reshape(db, 1, 128), state_conv, state_ssm,
      conv_w, conv_b[layer].reshape(1, -1), _pad_lanes(dt_bias[layer]), _pad_lanes(a_log[layer]),
      _pad_lanes(d_skip[layer]), norm_g[layer].reshape(1, -1))


def _mem_attn_prompt_kernel(q_ref, k_ref, v_ref, o_ref, kbf_ref, vbf_ref):
    @pl.when(pl.program_id(1) == 0)
    def _():
        kbf_ref[...] = k_ref[...].astype(BF16)
        vbf_ref[...] = v_ref[...].astype(BF16)

    dh = q_ref.shape[1] // N_HEADS_MEM
    dims_nt = (((1,), (1,)), ((), ()))
    for h in range(N_HEADS_MEM):
        sl = slice(h * dh, (h + 1) * dh)
        s = lax.dot_general(q_ref[:, sl], kbf_ref[:, sl], dims_nt, preferred_element_type=F32) * (dh ** -0.5)
        e = jnp.exp(s - jnp.max(s, axis=1, keepdims=True))
        p = e / jnp.sum(e, axis=1, keepdims=True)
        o_ref[:, sl] = jnp.dot(p.astype(BF16), vbf_ref[:, sl], preferred_element_type=F32).astype(o_ref.dtype)


def mem_attn_prompt(qm, kv, batch, seq, n_mem, tq=512):
    d = qm.shape[1]
    nq = seq // tq
    return pl.pallas_call(
        _mem_attn_prompt_kernel, grid=(batch, nq),
        in_specs=[pl.BlockSpec((tq, d), lambda b, i: (b * nq + i, 0)),
                  pl.BlockSpec((n_mem, d), lambda b, i: (b, 0)),
                  pl.BlockSpec((n_mem, d), lambda b, i: (b, 1))],
        out_specs=pl.BlockSpec((tq, d), lambda b, i: (b * nq + i, 0)),
        out_shape=jax.ShapeDtypeStruct((batch * seq, d), BF16),
        scratch_shapes=[pltpu.VMEM((n_mem, d), BF16), pltpu.VMEM((n_mem, d), BF16)],
        compiler_params=_cparams("parallel", "arbitrary"), name="mem_attn_prompt",
    )(qm, kv, kv)


def _mem_attn_decode_kernel(q_ref, k_ref, v_ref, o_ref):
    q = q_ref[...]
    s = jnp.sum(k_ref[...] * q[None], axis=-1, keepdims=True) * (q.shape[-1] ** -0.5)
    e = jnp.exp(s - jnp.max(s, axis=0, keepdims=True))
    p = e / jnp.sum(e, axis=0, keepdims=True)
    o_ref[...] = jnp.sum(p * v_ref[...], axis=0)


def mem_attn_decode(q, cache_mem_k, cache_mem_v, layer):
    db = q.shape[0]
    blk = cache_mem_k.shape[2:]
    tok = pl.BlockSpec((None,) + q.shape[1:], lambda b: (b, 0, 0))
    kv = pl.BlockSpec((None, None) + blk, lambda b: (layer, b, 0, 0, 0))
    return pl.pallas_call(
        _mem_attn_decode_kernel, grid=(db,), in_specs=[tok, kv, kv], out_specs=tok,
        out_shape=jax.ShapeDtypeStruct(q.shape, F32),
        compiler_params=_cparams("parallel"), name="mem_attn_decode",
    )(q, cache_mem_k, cache_mem_v)


def _router_kernel(x_ref, w_ref, idx_ref, gate_ref):
    logits = jnp.dot(x_ref[...], w_ref[...], preferred_element_type=F32, precision=lax.Precision.HIGHEST)
    n_exp = logits.shape[1]
    lane = lax.broadcasted_iota(jnp.int32, logits.shape, 1)
    m1 = jnp.max(logits, axis=1, keepdims=True)
    i1 = jnp.min(jnp.where(logits == m1, lane, n_exp), axis=1, keepdims=True)
    rest = jnp.where(lane == i1, -jnp.inf, logits)
    m2 = jnp.max(rest, axis=1, keepdims=True)
    i2 = jnp.min(jnp.where(rest == m2, lane, n_exp), axis=1, keepdims=True)
    e2 = jnp.exp(m2 - m1)
    idx_ref[:, 0:1] = i1
    idx_ref[:, 1:2] = i2
    gate_ref[:, 0:1] = 1.0 / (1.0 + e2)
    gate_ref[:, 1:2] = e2 / (1.0 + e2)


def router_top2(x, w_router, bm):
    m, d = x.shape
    n_exp = w_router.shape[-1]
    row = lambda w: pl.BlockSpec((bm, w), lambda i: (i, 0))
    return pl.pallas_call(
        _router_kernel, grid=(m // bm,),
        in_specs=[row(d), pl.BlockSpec((d, n_exp), lambda i: (0, 0))],
        out_specs=[row(TOP_K), row(TOP_K)],
        out_shape=[jax.ShapeDtypeStruct((m, TOP_K), jnp.int32), jax.ShapeDtypeStruct((m, TOP_K), F32)],
        compiler_params=_cparams("parallel"), name="router",
    )(x, w_router)


def _moe_plan(idx, n_exp, bm, n_tiles):
    e_flat = idx.reshape(-1)
    onehot = (e_flat[:, None] == jnp.arange(n_exp, dtype=jnp.int32)[None, :]).astype(jnp.int32)
    csum = jnp.cumsum(onehot, axis=0)
    counts = csum[-1]
    padded = ((counts + bm - 1) // bm) * bm
    gend = jnp.cumsum(padded)
    gstart = gend - padded
    dest = jnp.sum(onehot * (gstart[None, :] + csum - 1), axis=1).astype(jnp.int32)
    tile_start = jnp.arange(n_tiles, dtype=jnp.int32) * bm
    tile_expert = jnp.sum((tile_start[:, None] >= gend[None, :]).astype(jnp.int32), axis=1)
    n_valid = (gend[-1] // bm).astype(jnp.int32).reshape(1)
    return dest, jnp.minimum(tile_expert, n_exp - 1).astype(jnp.int32), n_valid


def _row_copy(src_ref, src_row, dst_ref, dst_row, sem):
    return pltpu.make_async_copy(src_ref.at[pl.ds(src_row, 1), :], dst_ref.at[pl.ds(dst_row, 1), :], sem)


def _moe_scatter_kernel(bm, dest_ref, x_ref, xs_in_ref, xs_ref, sem):
    del xs_in_ref
    base = pl.program_id(0) * bm * TOP_K

    def start(r, _):
        for k in range(TOP_K):
            _row_copy(x_ref, r, xs_ref, dest_ref[base + r * TOP_K + k], sem).start()
        return 0

    def wait(r, _):
        for k in range(TOP_K):
            _row_copy(x_ref, 0, xs_ref, 0, sem).wait()
        return 0

    lax.fori_loop(0, bm, start, 0)
    lax.fori_loop(0, bm, wait, 0)


def moe_scatter(x, dest, n_rows, bm):
    m, d = x.shape
    zeros = jnp.zeros((n_rows, d), x.dtype)
    return pl.pallas_call(
        functools.partial(_moe_scatter_kernel, bm),
        grid_spec=pltpu.PrefetchScalarGridSpec(
            num_scalar_prefetch=1, grid=(m // bm,),
            in_specs=[pl.BlockSpec((bm, d), lambda i, dest: (i, 0)), pl.BlockSpec(memory_space=pl.ANY)],
            out_specs=pl.BlockSpec(memory_space=pl.ANY),
            scratch_shapes=[pltpu.SemaphoreType.DMA(())]),
        out_shape=jax.ShapeDtypeStruct((n_rows, d), x.dtype),
        input_output_aliases={2: 0},
        compiler_params=_cparams("arbitrary"), name="moe_scatter",
    )(dest, x, zeros)


def _expert_changed(te_ref, i):
    return jnp.logical_or(i == 0, te_ref[i] != te_ref[jnp.maximum(i - 1, 0)])


def _gmm_up_kernel(te_ref, nv_ref, x_ref, wg_ref, wu_ref, o_ref, wgbf_ref, wubf_ref):
    i = pl.program_id(1)

    @pl.when(_expert_changed(te_ref, i))
    def _():
        wgbf_ref[...] = wg_ref[...].astype(BF16)
        wubf_ref[...] = wu_ref[...].astype(BF16)

    @pl.when(i < nv_ref[0])
    def _():
        x = x_ref[...].astype(BF16)
        g = jnp.dot(x, wgbf_ref[...], preferred_element_type=F32)
        u = jnp.dot(x, wubf_ref[...], preferred_element_type=F32)
        o_ref[...] = (g * jax.nn.sigmoid(g) * u).astype(o_ref.dtype)

    @pl.when(i >= nv_ref[0])
    def _():
        o_ref[...] = jnp.zeros_like(o_ref)


def gmm_up(xs, wg, wu, tile_expert, n_valid, bm, bn):
    rows, k = xs.shape
    f = wg.shape[-1]
    w_spec = pl.BlockSpec((None, None, k, bn), lambda j, i, te, nv: (0, te[i], 0, j))
    return pl.pallas_call(
        _gmm_up_kernel,
        grid_spec=pltpu.PrefetchScalarGridSpec(
            num_scalar_prefetch=2, grid=(f // bn, rows // bm),
            in_specs=[pl.BlockSpec((bm, k), lambda j, i, te, nv: (i, 0)), w_spec, w_spec],
            out_specs=pl.BlockSpec((bm, bn), lambda j, i, te, nv: (i, j)),
            scratch_shapes=[pltpu.VMEM((k, bn), BF16), pltpu.VMEM((k, bn), BF16)]),
        out_shape=jax.ShapeDtypeStruct((rows, f), BF16),
        compiler_params=_cparams("parallel", "arbitrary"), name="gmm_up",
    )(tile_expert, n_valid, xs, wg, wu)


def _gmm_down_kernel(te_ref, nv_ref, a_ref, w_ref, o_ref, wbf_ref):
    i = pl.program_id(1)

    @pl.when(_expert_changed(te_ref, i))
    def _():
        wbf_ref[...] = w_ref[...].astype(BF16)

    @pl.when(i < nv_ref[0])
    def _():
        o_ref[...] = jnp.dot(a_ref[...], wbf_ref[...], preferred_element_type=F32)

    @pl.when(i >= nv_ref[0])
    def _():
        o_ref[...] = jnp.zeros_like(o_ref)


def gmm_down(a, wd, tile_expert, n_valid, bm, bn):
    rows, f = a.shape
    d = wd.shape[-1]
    return pl.pallas_call(
        _gmm_down_kernel,
        grid_spec=pltpu.PrefetchScalarGridSpec(
            num_scalar_prefetch=2, grid=(d // bn, rows // bm),
            in_specs=[pl.BlockSpec((bm, f), lambda j, i, te, nv: (i, 0)),
                      pl.BlockSpec((None, None, f, bn), lambda j, i, te, nv: (0, te[i], 0, j),
                                   pipeline_mode=pl.Buffered(1))],
            out_specs=pl.BlockSpec((bm, bn), lambda j, i, te, nv: (i, j)),
            scratch_shapes=[pltpu.VMEM((f, bn), BF16)]),
        out_shape=jax.ShapeDtypeStruct((rows, d), F32),
        compiler_params=_cparams("parallel", "arbitrary"), name="gmm_down",
    )(tile_expert, n_valid, a, wd)


def _moe_combine_kernel(bm, alpha, dest_ref, x_ref, gate_ref, g_ref, b_ref, ys_ref, o_ref, obf_ref, buf_ref, sem):
    base = pl.program_id(0) * bm * TOP_K

    def start(r, _):
        for k in range(TOP_K):
            _row_copy(ys_ref, dest_ref[base + r * TOP_K + k], buf_ref.at[k], r, sem).start()
        return 0

    def wait(r, _):
        for k in range(TOP_K):
            _row_copy(ys_ref, 0, buf_ref.at[k], 0, sem).wait()
        return 0

    lax.fori_loop(0, bm, start, 0)
    lax.fori_loop(0, bm, wait, 0)
    f = gate_ref[:, 0:1] * buf_ref[0]
    for k in range(1, TOP_K):
        f = f + gate_ref[:, k:k + 1] * buf_ref[k]
    y = _ln_rows(alpha * x_ref[...] + f, g_ref[...], b_ref[...])
    o_ref[...] = y
    obf_ref[...] = y.astype(BF16)


def moe_combine(x, gates, ys, dest, g, b, alpha, bm):
    m, d = x.shape
    row = lambda w: pl.BlockSpec((bm, w), lambda i, dest: (i, 0))
    vec = pl.BlockSpec((1, d), lambda i, dest: (0, 0))
    return pl.pallas_call(
        functools.partial(_moe_combine_kernel, bm, alpha),
        grid_spec=pltpu.PrefetchScalarGridSpec(
            num_scalar_prefetch=1, grid=(m // bm,),
            in_specs=[row(d), row(TOP_K), vec, vec, pl.BlockSpec(memory_space=pl.ANY)],
            out_specs=[row(d), row(d)],
            scratch_shapes=[pltpu.VMEM((TOP_K, bm, d), F32), pltpu.SemaphoreType.DMA(())]),
        out_shape=[jax.ShapeDtypeStruct((m, d), F32), jax.ShapeDtypeStruct((m, d), BF16)],
        compiler_params=_cparams("arbitrary"), name="moe_combine",
    )(dest, x, gates, g.reshape(1, d), b.reshape(1, d), ys)


def moe_ffn_ln(x, w_router, wg, wu, wd, g, b, alpha, moe_layer):
    m, d = x.shape
    n_exp = w_router.shape[-1]
    bm_tok = _row_block(m, 320)
    bm = 512
    n_tiles = -(-(m * TOP_K + n_exp * (bm - 1)) // bm)
    idx, gates = router_top2(x, w_router[moe_layer], bm_tok)
    dest, tile_expert, n_valid = _moe_plan(idx, n_exp, bm, n_tiles)
    xs = moe_scatter(x, dest, n_tiles * bm, bm_tok)
    a = gmm_up(xs, wg[moe_layer:moe_layer + 1], wu[moe_layer:moe_layer + 1], tile_expert, n_valid, bm, 512)
    ys = gmm_down(a, wd[moe_layer:moe_layer + 1], tile_expert, n_valid, bm, 512)
    return moe_combine(x, gates, ys, dest, g, b, alpha, bm_tok)


def kernel(x_prompt, x_sample, mem_prompt, cache_k, cache_v, cache_mem_k, cache_mem_v, state_ssm, state_conv, page_table, ln_in_g, ln_in_b, w_in, conv_w, conv_b, dt_bias, a_log, d_skip, ssd_norm_g, lam_params, subln_g, w_out, w_mem_q, w_mem_kv, w_mem_o, ln_g, ln_b, w_ff_gate, w_ff_up, w_ff_down, w_router, w_exp_gate, w_exp_up, w_exp_down):
    batch, seq, d = x_prompt.shape
    db = x_sample.shape[0]
    assert x_sample.shape[1] == 1
    depth = w_in.shape[0]
    n_mem = mem_prompt.shape[1]
    mp = batch * seq
    m = mp + db
    alpha = (2 * depth) ** 0.25
    d_attn = N_HEADS_A * 128
    d_ssd = N_HEADS_S * 64
    cdim = d_ssd + 2 * SSD_GROUPS * SSD_STATE
    n_main = 3 * d_attn + d_ssd + cdim
    bm = _row_block(m, 1664)
    bm_small = _row_block(m, 416)

    x_all = jnp.concatenate([x_prompt.reshape(mp, d), x_sample.reshape(db, d)], axis=0)
    x, x_bf = layer_norm_in(x_all, ln_in_g, ln_in_b)
    mem2d = mem_prompt.reshape(batch * n_mem, d)

    outs = {k: [] for k in ("kp", "vp", "sp", "cp", "mkp", "mvp", "ks", "vs", "ss", "cs")}
    for l in range(depth):
        lam_init = 0.8 - 0.6 * math.exp(-0.3 * l)
        u = matmul(x_bf, w_in, lead=(l,), ncols=n_main, bm=bm, bn=512, name="mm_in")
        w_dt = jnp.pad(w_in[l][:, n_main:], ((0, 0), (0, 128 - N_HEADS_S)))
        dt_raw = matmul(x_bf, w_dt, bm=bm, bn=128, name="mm_dt")
        o_a = attn_prompt(u, lam_params, subln_g, l, batch, seq, lam_init)
        y_s, ssm_p, conv_p = ssd_prompt(u, dt_raw, conv_w, conv_b, dt_bias, a_log, d_skip, ssd_norm_g, l, batch, seq)
        us = u[mp:]
        heads = lambda a: a.reshape(db, N_HEADS_A, 128)
        o_a_s = attn_decode(heads(us[:, :d_attn]), heads(us[:, d_attn:2 * d_attn]), heads(us[:, 2 * d_attn:3 * d_attn]),
                            cache_k, cache_v, page_table, lam_params, subln_g, l, lam_init)
        y_s_s, ssm_s, conv_s = ssd_decode(us[:, n_main - cdim:n_main], us[:, 3 * d_attn:3 * d_attn + d_ssd], dt_raw[mp:],
                                          state_conv, state_ssm, conv_w, conv_b, dt_bias, a_log, d_skip, ssd_norm_g, l)
        mix = jnp.concatenate([jnp.concatenate([o_a, y_s], axis=1),
                               jnp.concatenate([o_a_s.reshape(db, d_attn), y_s_s.reshape(db, d_ssd)], axis=1).astype(BF16)],
                              axis=0)
        a = matmul(mix, w_out, lead=(l,), bm=bm, bn=512, name="mm_out")
        x, x_bf = add_layer_norm(x, a, ln_g[l, 0], ln_b[l, 0], alpha)
        qm = matmul(x_bf, w_mem_q, lead=(l,), bm=bm, bn=512, out_dtype=BF16, name="mm_mem_q")
        kv = matmul(mem2d, w_mem_kv, lead=(l,), bm=_row_block(batch * n_mem, 512), bn=512, name="mm_mem_kv")
        c_p = mem_attn_prompt(qm, kv, batch, seq, n_mem)
        c_s = mem_attn_decode(qm[mp:].astype(F32).reshape(db, N_HEADS_MEM, d // N_HEADS_MEM), cache_mem_k, cache_mem_v, l)
        c = jnp.concatenate([c_p, c_s.reshape(db, d).astype(BF16)], axis=0)
        a = matmul(c, w_mem_o, lead=(l,), bm=bm, bn=512, name="mm_mem_o")
        x, x_bf = add_layer_norm(x, a, ln_g[l, 1], ln_b[l, 1], alpha)
        if l % 2 == 0:
            act = swiglu_up(x_bf, w_ff_gate, w_ff_up, lead=(l // 2,), bm=bm, bn=512)
            a = matmul(act, w_ff_down, lead=(l // 2,), bm=bm_small, bn=512, name="mm_ff_down")
            x, x_bf = add_layer_norm(x, a, ln_g[l, 2], ln_b[l, 2], alpha)
        else:
            x, x_bf = moe_ffn_ln(x, w_router, w_exp_gate, w_exp_up, w_exp_down, ln_g[l, 2], ln_b[l, 2], alpha, l // 2)
        outs["kp"].append(u[:mp, d_attn:2 * d_attn].reshape(batch, seq, N_HEADS_A, 128))
        outs["vp"].append(u[:mp, 2 * d_attn:3 * d_attn].reshape(batch, seq, N_HEADS_A, 128))
        outs["sp"].append(ssm_p.reshape(batch, N_HEADS_S, 64, SSD_STATE))
        outs["cp"].append(conv_p[:, 8 - (CONV_K - 1):, :])
        outs["mkp"].append(kv[:, :d].reshape(batch, n_mem, N_HEADS_MEM, d // N_HEADS_MEM))
        outs["mvp"].append(kv[:, d:].reshape(batch, n_mem, N_HEADS_MEM, d // N_HEADS_MEM))
        outs["ks"].append(us[:, d_attn:2 * d_attn].reshape(db, 1, N_HEADS_A, 128))
        outs["vs"].append(us[:, 2 * d_attn:3 * d_attn].reshape(db, 1, N_HEADS_A, 128))
        outs["ss"].append(ssm_s)
        outs["cs"].append(conv_s)
    st = {k: jnp.stack(v) for k, v in outs.items()}
    return (x[:mp].reshape(batch, seq, d), x[mp:].reshape(db, 1, d),
            st["kp"], st["vp"], st["sp"], st["cp"], st["mkp"], st["mvp"], st["ks"], st["vs"], st["ss"], st["cs"])
```

```python
import functools
import math

import jax
import jax.numpy as jnp
from jax import lax
from jax.experimental import pallas as pl
from jax.experimental.pallas import tpu as pltpu

F32 = jnp.float32
BF16 = jnp.bfloat16

LN_EPS = 1e-5
RMS_EPS = 1e-5
N_HEADS_A = 8
QK_DIM = 64
N_HEADS_S = 16
SSD_GROUPS = 2
SSD_STATE = 128
SSD_CHUNK = 128
CONV_K = 4
N_HEADS_MEM = 4
TOP_K = 2
SSD_DECODE_SEQS = 4
VMEM_LIMIT = 56 * 1024 * 1024


def _cparams(*sem):
    return pltpu.CompilerParams(dimension_semantics=sem, vmem_limit_bytes=VMEM_LIMIT)


def _row_block(m, target):
    best = None
    for d in range(16, min(m, target) + 1, 16):
        if m % d == 0:
            best = d
    assert best is not None, (m, target)
    return best


def _ln_rows(x, g, b):
    mu = jnp.mean(x, axis=-1, keepdims=True)
    xc = x - mu
    var = jnp.mean(xc * xc, axis=-1, keepdims=True)
    return xc * lax.rsqrt(var + LN_EPS) * g + b


def _ln_kernel(x_ref, g_ref, b_ref, o_ref, obf_ref):
    y = _ln_rows(x_ref[...], g_ref[...], b_ref[...])
    o_ref[...] = y
    obf_ref[...] = y.astype(BF16)


def layer_norm_in(x, g, b):
    m, d = x.shape
    bm = _row_block(m, 512)
    row = pl.BlockSpec((bm, d), lambda i: (i, 0))
    vec = pl.BlockSpec((1, d), lambda i: (0, 0))
    return pl.pallas_call(
        _ln_kernel, grid=(m // bm,), in_specs=[row, vec, vec], out_specs=[row, row],
        out_shape=[jax.ShapeDtypeStruct((m, d), F32), jax.ShapeDtypeStruct((m, d), BF16)],
        compiler_params=_cparams("parallel"), name="ln_in",
    )(x, g.reshape(1, d), b.reshape(1, d))


def _add_ln_kernel(alpha, x_ref, a_ref, g_ref, b_ref, o_ref, obf_ref):
    y = _ln_rows(alpha * x_ref[...] + a_ref[...], g_ref[...], b_ref[...])
    o_ref[...] = y
    obf_ref[...] = y.astype(BF16)


def add_layer_norm(x, a, g, b, alpha):
    m, d = x.shape
    bm = _row_block(m, 512)
    row = pl.BlockSpec((bm, d), lambda i: (i, 0))
    vec = pl.BlockSpec((1, d), lambda i: (0, 0))
    return pl.pallas_call(
        functools.partial(_add_ln_kernel, alpha), grid=(m // bm,),
        in_specs=[row, row, vec, vec], out_specs=[row, row],
        out_shape=[jax.ShapeDtypeStruct((m, d), F32), jax.ShapeDtypeStruct((m, d), BF16)],
        compiler_params=_cparams("parallel"), name="add_ln",
    )(x, a, g.reshape(1, d), b.reshape(1, d))


def _mm_kernel(x_ref, w_ref, o_ref, wbf_ref):
    @pl.when(pl.program_id(1) == 0)
    def _():
        wbf_ref[...] = w_ref[...].astype(BF16)

    o_ref[...] = jnp.dot(x_ref[...].astype(BF16), wbf_ref[...],
                         preferred_element_type=F32).astype(o_ref.dtype)


def matmul(x, w, *, lead=(), col0=0, ncols=None, bm, bn, out_dtype=F32, name="mm"):
    m, k = x.shape
    n = w.shape[-1] - col0 if ncols is None else ncols
    assert w.shape[-2] == k and m % bm == 0 and n % bn == 0 and col0 % bn == 0
    nl = len(lead)
    cb0 = col0 // bn
    w_spec = pl.BlockSpec((None,) * nl + (k, bn), lambda j, i: tuple(lead) + (0, cb0 + j))
    return pl.pallas_call(
        _mm_kernel, grid=(n // bn, m // bm),
        in_specs=[pl.BlockSpec((bm, k), lambda j, i: (i, 0)), w_spec],
        out_specs=pl.BlockSpec((bm, bn), lambda j, i: (i, j)),
        out_shape=jax.ShapeDtypeStruct((m, n), out_dtype),
        scratch_shapes=[pltpu.VMEM((k, bn), BF16)],
        compiler_params=_cparams("parallel", "arbitrary"), name=name,
    )(x, w)


def _mm_add_ln_kernel(alpha, x_ref, w_ref, r_ref, g_ref, b_ref, o_ref, obf_ref, wbf_ref):
    @pl.when(pl.program_id(0) == 0)
    def _():
        wbf_ref[...] = w_ref[...].astype(BF16)

    a = jnp.dot(x_ref[...], wbf_ref[...], preferred_element_type=F32)
    y = _ln_rows(alpha * r_ref[...] + a, g_ref[...], b_ref[...])
    o_ref[...] = y
    obf_ref[...] = y.astype(BF16)


def matmul_add_ln(x, w, resid, g, b, alpha, *, lead, bm, name):
    m, k = x.shape
    n = w.shape[-1]
    assert m % bm == 0 and resid.shape == (m, n)
    row = lambda width: pl.BlockSpec((bm, width), lambda i: (i, 0))
    vec = pl.BlockSpec((1, n), lambda i: (0, 0))
    w_spec = pl.BlockSpec((None,) * len(lead) + (k, n), lambda i: tuple(lead) + (0, 0), pipeline_mode=pl.Buffered(1))
    return pl.pallas_call(
        functools.partial(_mm_add_ln_kernel, alpha), grid=(m // bm,),
        in_specs=[row(k), w_spec, row(n), vec, vec], out_specs=[row(n), row(n)],
        out_shape=[jax.ShapeDtypeStruct((m, n), F32), jax.ShapeDtypeStruct((m, n), BF16)],
        scratch_shapes=[pltpu.VMEM((k, n), BF16)],
        compiler_params=_cparams("arbitrary"), name=name,
    )(x, w, resid, g.reshape(1, n), b.reshape(1, n))


def _swiglu_up_kernel(x_ref, wg_ref, wu_ref, o_ref, wgbf_ref, wubf_ref):
    @pl.when(pl.program_id(1) == 0)
    def _():
        wgbf_ref[...] = wg_ref[...].astype(BF16)
        wubf_ref[...] = wu_ref[...].astype(BF16)

    x = x_ref[...]
    g = jnp.dot(x, wgbf_ref[...], preferred_element_type=F32)
    u = jnp.dot(x, wubf_ref[...], preferred_element_type=F32)
    o_ref[...] = (g * jax.nn.sigmoid(g) * u).astype(o_ref.dtype)


def swiglu_up(x, wg, wu, *, lead, bm, bn):
    m, k = x.shape
    n = wg.shape[-1]
    assert m % bm == 0 and n % bn == 0
    nl = len(lead)
    w_spec = pl.BlockSpec((None,) * nl + (k, bn), lambda j, i: tuple(lead) + (0, j))
    return pl.pallas_call(
        _swiglu_up_kernel, grid=(n // bn, m // bm),
        in_specs=[pl.BlockSpec((bm, k), lambda j, i: (i, 0)), w_spec, w_spec],
        out_specs=pl.BlockSpec((bm, bn), lambda j, i: (i, j)),
        out_shape=jax.ShapeDtypeStruct((m, n), BF16),
        scratch_shapes=[pltpu.VMEM((k, bn), BF16), pltpu.VMEM((k, bn), BF16)],
        compiler_params=_cparams("parallel", "arbitrary"), name="swiglu_up",
    )(x, wg, wu)


def _lambda_value(lp, lam_init):
    t1 = jnp.sum(lp[0:1, :] * lp[1:2, :], axis=1, keepdims=True)
    t2 = jnp.sum(lp[2:3, :] * lp[3:4, :], axis=1, keepdims=True)
    return jnp.exp(t1) - jnp.exp(t2) + lam_init


LOG2E = 1.4426950408889634


def _attn_prompt_kernel(tq, lam_init, q_ref, k_ref, v_ref, lp_ref, g_ref, o_ref, kbf_ref, vt_ref):
    h = pl.program_id(1)
    qi = pl.program_id(2)

    @pl.when(qi == 0)
    def _():
        kbf_ref[...] = k_ref[...].astype(BF16)
        for c in range(vt_ref.shape[0]):
            vt_ref[c] = v_ref[c * tq:(c + 1) * tq, :].T.astype(BF16)

    slope = jnp.exp2(-(h + 1).astype(F32) * jnp.ones((1, 1), F32)) * LOG2E
    lam = _lambda_value(lp_ref[...], lam_init)
    qt = (q_ref[...] * (QK_DIM ** -0.5 * LOG2E)).T
    sub = lax.broadcasted_iota(jnp.int32, qt.shape, 0)
    qts = (jnp.where(sub < QK_DIM, qt, 0.0).astype(BF16), jnp.where(sub >= QK_DIM, qt, 0.0).astype(BF16))
    krow = lax.broadcasted_iota(jnp.int32, (tq, tq), 0)
    qcol = lax.broadcasted_iota(jnp.int32, (tq, tq), 1)
    base = -slope * (qcol - krow).astype(F32)

    def chunk(kj, carry, masked):
        kc = kbf_ref[pl.ds(pl.multiple_of(kj * tq, tq), tq), :]
        vtc = vt_ref[kj]
        off = -slope * ((qi - kj) * tq).astype(F32)
        out = []
        for qm, (m, l, acc) in zip(qts, carry):
            t = jnp.dot(kc, qm, preferred_element_type=F32) + base
            if masked:
                t = jnp.where(krow <= qcol, t, -jnp.inf)
            m_new = jnp.maximum(m, jnp.max(t, axis=0, keepdims=True) + off)
            p = jnp.exp2(t - (m_new - off))
            alpha = jnp.exp2(m - m_new)
            l_new = alpha * l + jnp.sum(p, axis=0, keepdims=True)
            acc_new = alpha * acc + jnp.dot(vtc, p.astype(BF16), preferred_element_type=F32)
            out.append((m_new, l_new, acc_new))
        return tuple(out)

    init_one = (jnp.full((1, tq), -1e30, F32), jnp.zeros((1, tq), F32), jnp.zeros((128, tq), F32))
    carry = lax.fori_loop(0, qi, lambda kj, c: chunk(kj, c, False), (init_one, init_one))
    (_, l1, a1), (_, l2, a2) = chunk(qi, carry, True)
    o = (a1 / l1 - lam * (a2 / l2)).T
    o = o * lax.rsqrt(jnp.mean(o * o, axis=1, keepdims=True) + RMS_EPS)
    o_ref[...] = (o * g_ref[...] * (1.0 - lam_init)).astype(o_ref.dtype)


def attn_prompt(u, lam_params, subln_g, layer, batch, seq, lam_init, tq=512):
    nq = seq // tq
    lp_spec = pl.BlockSpec((None, 4, QK_DIM), lambda b, h, i: (layer, 0, 0))
    g_spec = pl.BlockSpec((None, 1, 128), lambda b, h, i: (layer, 0, 0))
    return pl.pallas_call(
        functools.partial(_attn_prompt_kernel, tq, lam_init),
        grid=(batch, N_HEADS_A, nq),
        in_specs=[pl.BlockSpec((tq, 128), lambda b, h, i: (b * nq + i, h)),
                  pl.BlockSpec((seq, 128), lambda b, h, i: (b, N_HEADS_A + h)),
                  pl.BlockSpec((seq, 128), lambda b, h, i: (b, 2 * N_HEADS_A + h)),
                  lp_spec, g_spec],
        out_specs=pl.BlockSpec((tq, 128), lambda b, h, i: (b * nq + i, h)),
        out_shape=jax.ShapeDtypeStruct((batch * seq, N_HEADS_A * 128), BF16),
        scratch_shapes=[pltpu.VMEM((seq, 128), BF16), pltpu.VMEM((nq, 128, tq), BF16)],
        compiler_params=_cparams("parallel", "parallel", "arbitrary"), name="attn_prompt",
    )(u, u, u, lam_params, subln_g.reshape(-1, 1, 128))


def _softplus(x):
    return jnp.maximum(x, 0.0) + jnp.log1p(jnp.exp(-jnp.abs(x)))


def _silu(x):
    return x * jax.nn.sigmoid(x)


def _ssd_prompt_kernel(xs_ref, bc_ref, z_ref, dt_ref, cw_ref, cb_ref, dtb_ref, alog_ref, dsk_ref, ng_ref,
                       y_ref, st_ref, conv_ref, xp_ref, h_ref):
    c = pl.program_id(1)
    nc = pl.num_programs(1)
    q = SSD_CHUNK
    d_ssd = N_HEADS_S * 64

    @pl.when(c == 0)
    def _():
        xp_ref[0:8, :] = jnp.zeros((8, xp_ref.shape[1]), F32)
        h_ref[...] = jnp.zeros_like(h_ref)

    xp_ref[8:8 + q, 0:d_ssd] = xs_ref[...]
    xp_ref[8:8 + q, d_ssd:] = bc_ref[...]
    cw = cw_ref[...]
    xc = cb_ref[...] + cw[3:4, :] * xp_ref[8:8 + q, :]
    for j in range(1, CONV_K):
        xc = xc + cw[3 - j:4 - j, :] * xp_ref[8 - j:8 - j + q, :]
    xp_ref[0:8, :] = xp_ref[q:q + 8, :]
    xc = _silu(xc)
    xs = xc[:, :d_ssd]

    dt = _softplus(dt_ref[...] + dtb_ref[...])
    a_neg = -jnp.exp(alog_ref[...])
    da = dt * a_neg
    ri = lax.broadcasted_iota(jnp.int32, (q, q), 0)
    ci = lax.broadcasted_iota(jnp.int32, (q, q), 1)
    causal = ci <= ri
    tril = jnp.where(causal, 1.0, 0.0).astype(F32)
    a_cs = jnp.dot(tril, da, preferred_element_type=F32, precision=lax.Precision.HIGHEST)
    a_cs_t = a_cs.T
    a_last = a_cs[q - 1:q, :]
    e_cs = jnp.exp(a_cs)
    e_end = jnp.exp(a_last - a_cs)
    e_last = jnp.exp(a_last)
    lane = lax.broadcasted_iota(jnp.int32, (q, 128), 1)
    lo = lane < 64
    rsel = lax.broadcasted_iota(jnp.int32, (128, SSD_STATE), 0) < 64
    dims_nt = (((1,), (1,)), ((), ()))
    dims_tn = (((0,), (0,)), ((), ()))
    hpg = N_HEADS_S // SSD_GROUPS

    ys = []
    for g in range(SSD_GROUPS):
        bm_g = xc[:, d_ssd + g * SSD_STATE:d_ssd + (g + 1) * SSD_STATE].astype(BF16)
        cm_g = xc[:, d_ssd + (SSD_GROUPS + g) * SSD_STATE:d_ssd + (SSD_GROUPS + g + 1) * SSD_STATE].astype(BF16)
        cb = lax.dot_general(cm_g, bm_g, dims_nt, preferred_element_type=F32)
        for pr in range(hpg // 2):
            h0 = g * hpg + 2 * pr
            x_pair = xs[:, h0 * 64:h0 * 64 + 128]
            dt_pair = jnp.where(lo, dt[:, h0:h0 + 1], dt[:, h0 + 1:h0 + 2])
            xdt = x_pair * dt_pair
            y_pair = jnp.zeros((q, 128), F32)
            for k, keep in ((0, lo), (1, jnp.logical_not(lo))):
                hh = h0 + k
                seg = a_cs[:, hh:hh + 1] - a_cs_t[hh:hh + 1, :]
                decay = jnp.exp(jnp.where(causal, seg, -jnp.inf))
                mat = (cb * decay).astype(BF16)
                y_pair = y_pair + jnp.dot(mat, jnp.where(keep, xdt, 0.0).astype(BF16), preferred_element_type=F32)
            end_pair = jnp.where(lo, e_end[:, h0:h0 + 1], e_end[:, h0 + 1:h0 + 2])
            cs_pair = jnp.where(lo, e_cs[:, h0:h0 + 1], e_cs[:, h0 + 1:h0 + 2])
            h_prev = h_ref[h0 * 64:h0 * 64 + 128, :]
            y_off = lax.dot_general(cm_g, h_prev.astype(BF16), dims_nt, preferred_element_type=F32) * cs_pair
            st = lax.dot_general((xdt * end_pair).astype(BF16), bm_g, dims_tn, preferred_element_type=F32)
            dec = jnp.where(rsel, e_last[:, h0:h0 + 1], e_last[:, h0 + 1:h0 + 2])
            h_ref[h0 * 64:h0 * 64 + 128, :] = dec * h_prev + st
            dsk_pair = jnp.where(lo[0:1, :], dsk_ref[:, h0:h0 + 1], dsk_ref[:, h0 + 1:h0 + 2])
            ys.append(y_pair + y_off + dsk_pair * x_pair)
    y = jnp.concatenate(ys, axis=1)
    gz = y * _silu(z_ref[...])
    half = d_ssd // SSD_GROUPS
    outs = []
    for g in range(SSD_GROUPS):
        part = gz[:, g * half:(g + 1) * half]
        outs.append(part * lax.rsqrt(jnp.mean(part * part, axis=1, keepdims=True) + RMS_EPS))
    y_ref[...] = (jnp.concatenate(outs, axis=1) * ng_ref[...]).astype(y_ref.dtype)

    @pl.when(c == nc - 1)
    def _():
        st_ref[...] = h_ref[...]
        conv_ref[:, 0:d_ssd] = xs_ref[q - 8:q, :]
        conv_ref[:, d_ssd:] = bc_ref[q - 8:q, :]


def _pad_lanes(v):
    return jnp.pad(v.reshape(1, -1), ((0, 0), (0, 128 - v.shape[-1])))


def ssd_prompt(u, dt_raw, conv_w, conv_b, dt_bias, a_log, d_skip, norm_g, layer, batch, seq):
    q = SSD_CHUNK
    nc = seq // q
    d_ssd = N_HEADS_S * 64
    d_bc = 2 * SSD_GROUPS * SSD_STATE
    cdim = d_ssd + d_bc
    d_attn = N_HEADS_A * 128
    z0, x0, bc0 = 3 * d_attn, 3 * d_attn + d_ssd, 3 * d_attn + 2 * d_ssd
    assert z0 % d_ssd == 0 and x0 % d_ssd == 0 and bc0 % d_bc == 0
    row = lambda b, c: (b * nc + c, 0)
    vec = lambda width: pl.BlockSpec((1, width), lambda b, c: (0, 0))
    return pl.pallas_call(
        _ssd_prompt_kernel, grid=(batch, nc),
        in_specs=[pl.BlockSpec((q, d_ssd), lambda b, c: (b * nc + c, x0 // d_ssd)),
                  pl.BlockSpec((q, d_bc), lambda b, c: (b * nc + c, bc0 // d_bc)),
                  pl.BlockSpec((q, d_ssd), lambda b, c: (b * nc + c, z0 // d_ssd)),
                  pl.BlockSpec((q, 128), row),
                  pl.BlockSpec((None, CONV_K, cdim), lambda b, c: (layer, 0, 0)),
                  vec(cdim), vec(128), vec(128), vec(128), vec(d_ssd)],
        out_specs=[pl.BlockSpec((q, d_ssd), row),
                   pl.BlockSpec((None, N_HEADS_S * 64, SSD_STATE), lambda b, c: (b, 0, 0)),
                   pl.BlockSpec((None, 8, cdim), lambda b, c: (b, 0, 0))],
        out_shape=[jax.ShapeDtypeStruct((batch * seq, d_ssd), BF16),
                   jax.ShapeDtypeStruct((batch, N_HEADS_S * 64, SSD_STATE), F32),
                   jax.ShapeDtypeStruct((batch, 8, cdim), F32)],
        scratch_shapes=[pltpu.VMEM((q + 8, cdim), F32), pltpu.VMEM((N_HEADS_S * 64, SSD_STATE), F32)],
        compiler_params=_cparams("parallel", "arbitrary"), name="ssd_prompt",
    )(u, u, u, dt_raw, conv_w, conv_b[layer].reshape(1, -1), _pad_lanes(dt_bias[layer]),
      _pad_lanes(a_log[layer]), _pad_lanes(d_skip[layer]), norm_g[layer].reshape(1, -1))


def _kv_heads_kernel(k0_ref, v0_ref, k1_ref, v1_ref, ko_ref, vo_ref):
    def emit(k_ref, v_ref):
        for h in range(N_HEADS_A):
            ko_ref[:, h, :] = k_ref[:, h * 128:(h + 1) * 128]
            vo_ref[:, h, :] = v_ref[:, h * 128:(h + 1) * 128]

    @pl.when(pl.program_id(0) == 0)
    def _():
        emit(k0_ref, v0_ref)

    @pl.when(pl.program_id(0) == 1)
    def _():
        emit(k1_ref, v1_ref)


def kv_prompt_outputs(us, rows, tl=512):
    assert len(us) == 2 and rows % tl == 0
    nb = rows // tl
    d_attn = N_HEADS_A * 128

    def col(which, layer):
        idle = nb - 1 if layer == 0 else 0
        return pl.BlockSpec((tl, d_attn), lambda l, i: (jnp.where(l == layer, i, idle), which))

    out = pl.BlockSpec((None, tl, N_HEADS_A, 128), lambda l, i: (l, i, 0, 0))
    shape = jax.ShapeDtypeStruct((2, rows, N_HEADS_A, 128), F32)
    return pl.pallas_call(
        _kv_heads_kernel, grid=(2, nb),
        in_specs=[col(1, 0), col(2, 0), col(1, 1), col(2, 1)], out_specs=[out, out], out_shape=[shape, shape],
        compiler_params=_cparams("arbitrary", "arbitrary"), name="kv_heads",
    )(us[0], us[0], us[1], us[1])


def _attn_decode_kernel(n_pages, page, lam_init, pt_ref, q_ref, kn_ref, vn_ref, lp_ref, g_ref, *refs):
    k_refs, v_refs = refs[:n_pages], refs[n_pages:2 * n_pages]
    o_ref, s_ref = refs[2 * n_pages], refs[2 * n_pages + 1]
    lam = _lambda_value(lp_ref[...], lam_init)
    q = q_ref[...] * (QK_DIM ** -0.5 * LOG2E)
    lo = lax.broadcasted_iota(jnp.int32, q.shape, 1) < QK_DIM
    slope = jnp.exp2(-(lax.broadcasted_iota(jnp.int32, q.shape, 0) + 1).astype(F32)) * LOG2E
    tok_bias = slope[None] * lax.broadcasted_iota(jnp.int32, (page, 1, 1), 0).astype(F32)
    past = n_pages * page

    ri = lax.broadcasted_iota(jnp.int32, (128, 128), 0) < QK_DIM
    ci = lax.broadcasted_iota(jnp.int32, (128, 128), 1) < QK_DIM
    half_sum = jnp.where(ri == ci, 1.0, 0.0).astype(BF16)

    def packed_scores(k):
        prod = (k * q).reshape(-1, 128).astype(BF16)
        return jnp.dot(prod, half_sum, preferred_element_type=F32).reshape(k.shape)

    s_new = packed_scores(kn_ref[...])
    m = s_new
    page_bias = [slope * float(past - p * page) for p in range(n_pages)]
    for p in range(n_pages):
        s = packed_scores(k_refs[p][...]) + tok_bias
        s_ref[p * page:(p + 1) * page] = s
        m = jnp.maximum(m, jnp.max(s, axis=0) - page_bias[p])
    e_new = jnp.exp2(s_new - m)
    l = e_new
    for p in range(n_pages):
        e = jnp.exp2(s_ref[p * page:(p + 1) * page] - (m + page_bias[p])[None])
        s_ref[p * page:(p + 1) * page] = e
        l = l + jnp.sum(e, axis=0)
    r = 1.0 / l
    coef = jnp.where(lo, r, -lam * r)
    w_new = e_new * coef
    acc = (w_new + pltpu.roll(w_new, QK_DIM, 1)) * vn_ref[...]
    for p in range(n_pages):
        w = s_ref[p * page:(p + 1) * page] * coef[None]
        w = w + pltpu.roll(w, QK_DIM, 2)
        acc = acc + jnp.sum(w * v_refs[p][...], axis=0)
    o = acc * lax.rsqrt(jnp.mean(acc * acc, axis=-1, keepdims=True) + RMS_EPS)
    o_ref[...] = o * g_ref[...] * (1.0 - lam_init)


def attn_decode(q, k_new, v_new, cache_k, cache_v, page_table, lam_params, subln_g, layer, lam_init):
    db, n_pages = page_table.shape
    page = cache_k.shape[2]

    def kv_spec(j):
        return pl.BlockSpec((None, None, page, N_HEADS_A, 128), lambda b, pt: (layer, pt[b, j], 0, 0, 0))

    tok = pl.BlockSpec((None, N_HEADS_A, 128), lambda b, pt: (b, 0, 0))
    in_specs = [tok, tok, tok,
                pl.BlockSpec((None, 4, QK_DIM), lambda b, pt: (layer, 0, 0)),
                pl.BlockSpec((None, 1, 128), lambda b, pt: (layer, 0, 0))]
    in_specs += [kv_spec(j) for j in range(n_pages)] * 2
    return pl.pallas_call(
        functools.partial(_attn_decode_kernel, n_pages, page, lam_init),
        grid_spec=pltpu.PrefetchScalarGridSpec(
            num_scalar_prefetch=1, grid=(db,), in_specs=in_specs, out_specs=tok,
            scratch_shapes=[pltpu.VMEM((n_pages * page, N_HEADS_A, 128), F32)]),
        out_shape=jax.ShapeDtypeStruct((db, N_HEADS_A, 128), F32),
        compiler_params=_cparams("parallel"), name="attn_decode",
    )(page_table, q, k_new, v_new, lam_params, subln_g.reshape(-1, 1, 128), *([cache_k] * n_pages),
      *([cache_v] * n_pages))


def _ssd_decode_kernel(xbc_ref, z_ref, dt_ref, cst_ref, ssm_ref, cw_ref, cb_ref, dtb_ref, alog_ref, dsk_ref,
                       ng_ref, y_ref, ssm_out_ref, conv_out_ref):
    for s in range(xbc_ref.shape[0]):
        _ssd_decode_one(xbc_ref.at[s], z_ref.at[s], dt_ref.at[s], cst_ref.at[s], ssm_ref.at[s], cw_ref, cb_ref, dtb_ref,
                        alog_ref, dsk_ref, ng_ref, y_ref.at[s], ssm_out_ref.at[s], conv_out_ref.at[s])


def _ssd_decode_one(xbc_ref, z_ref, dt_ref, cst_ref, ssm_ref, cw_ref, cb_ref, dtb_ref, alog_ref, dsk_ref,
                    ng_ref, y_ref, ssm_out_ref, conv_out_ref):
    d_ssd = N_HEADS_S * 64
    hpg = N_HEADS_S // SSD_GROUPS
    xnew = xbc_ref[...]
    cst = cst_ref[...]
    cw = cw_ref[...]
    xc = cb_ref[...] + cw[3:4, :] * xnew
    for j in range(CONV_K - 1):
        xc = xc + cw[j:j + 1, :] * cst[j:j + 1, :]
    xc = _silu(xc)
    conv_out_ref[0:2, :] = cst[1:3, :]
    conv_out_ref[2:3, :] = xnew

    dt = _softplus(dt_ref[...] + dtb_ref[...])
    dec = jnp.exp(dt * (-jnp.exp(alog_ref[...])))
    eye = lax.broadcasted_iota(jnp.int32, (128, 128), 0) == lax.broadcasted_iota(jnp.int32, (128, 128), 1)
    lo = lax.broadcasted_iota(jnp.int32, (1, 128), 1) < 64
    dims_nt = (((1,), (1,)), ((), ()))
    ys = []
    for pr in range(N_HEADS_S // 2):
        h0 = 2 * pr
        g = h0 // hpg
        x_pair = xc[:, h0 * 64:h0 * 64 + 128]
        b_g = xc[:, d_ssd + g * SSD_STATE:d_ssd + (g + 1) * SSD_STATE]
        c_g = xc[:, d_ssd + (SSD_GROUPS + g) * SSD_STATE:d_ssd + (SSD_GROUPS + g + 1) * SSD_STATE]
        dt_pair = jnp.where(lo, dt[:, h0:h0 + 1], dt[:, h0 + 1:h0 + 2])
        xdt_row = x_pair * dt_pair
        xdt_diag = jnp.where(eye, jnp.broadcast_to(xdt_row, (128, 128)), 0.0).astype(BF16)
        upd = jnp.dot(xdt_diag, jnp.broadcast_to(b_g, (128, SSD_STATE)).astype(BF16), preferred_element_type=F32)
        h_prev = ssm_ref[h0:h0 + 2]
        h_new = jnp.concatenate([jnp.broadcast_to(dec[:, h0 + k:h0 + k + 1], (64, SSD_STATE)) * h_prev[k]
                                 for k in range(2)], axis=0) + upd
        ssm_out_ref[h0:h0 + 2] = h_new.reshape(2, 64, SSD_STATE)
        y_rows = lax.dot_general(jnp.broadcast_to(c_g, (8, SSD_STATE)).astype(BF16), h_new.astype(BF16), dims_nt,
                                 preferred_element_type=F32)
        dsk_pair = jnp.where(lo, dsk_ref[:, h0:h0 + 1], dsk_ref[:, h0 + 1:h0 + 2])
        ys.append(y_rows[0:1, :] + dsk_pair * x_pair)
    y = jnp.concatenate(ys, axis=1)
    gz = y * _silu(z_ref[...])
    half = d_ssd // SSD_GROUPS
    outs = []
    for g in range(SSD_GROUPS):
        part = gz[:, g * half:(g + 1) * half]
        outs.append(part * lax.rsqrt(jnp.mean(part * part, axis=1, keepdims=True) + RMS_EPS))
    y_ref[...] = jnp.concatenate(outs, axis=1) * ng_ref[...]


def ssd_decode(xbc, z, dt_raw, state_conv, state_ssm, conv_w, conv_b, dt_bias, a_log, d_skip, norm_g, layer):
    db, cdim = xbc.shape
    d_ssd = N_HEADS_S * 64
    bb = SSD_DECODE_SEQS
    assert db % bb == 0
    one = lambda width: pl.BlockSpec((bb, 1, width), lambda b: (b, 0, 0))
    vec = lambda width: pl.BlockSpec((1, width), lambda b: (0, 0))
    ssm_shape = state_ssm.shape[2:]
    return pl.pallas_call(
        _ssd_decode_kernel, grid=(db // bb,),
        in_specs=[one(cdim), one(d_ssd), one(128),
                  pl.BlockSpec((None, bb, CONV_K - 1, cdim), lambda b: (layer, b, 0, 0)),
                  pl.BlockSpec((None, bb) + ssm_shape, lambda b: (layer, b, 0, 0, 0)),
                  pl.BlockSpec((None, CONV_K, cdim), lambda b: (layer, 0, 0)),
                  vec(cdim), vec(128), vec(128), vec(128), vec(d_ssd)],
        out_specs=[one(d_ssd),
                   pl.BlockSpec((bb,) + ssm_shape, lambda b: (b, 0, 0, 0)),
                   pl.BlockSpec((bb, CONV_K - 1, cdim), lambda b: (b, 0, 0))],
        out_shape=[jax.ShapeDtypeStruct((db, 1, d_ssd), F32),
                   jax.ShapeDtypeStruct((db,) + ssm_shape, F32),
                   jax.ShapeDtypeStruct((db, CONV_K - 1, cdim), F32)],
        compiler_params=_cparams("parallel"), name="ssd_decode",
    )(xbc.reshape(db, 1, cdim), z.reshape(db, 1, d_ssd), dt_raw.reshape(db, 1, 128), state_conv, state_ssm,
      conv_w, conv_b[layer].reshape(1, -1), _pad_lanes(dt_bias[layer]), _pad_lanes(a_log[layer]),
      _pad_lanes(d_skip[layer]), norm_g[layer].reshape(1, -1))


def _mem_attn_prompt_kernel(q_ref, k_ref, v_ref, o_ref, kbf_ref, vbf_ref):
    @pl.when(pl.program_id(1) == 0)
    def _():
        kbf_ref[...] = k_ref[...].astype(BF16)
        vbf_ref[...] = v_ref[...].astype(BF16)

    dh = q_ref.shape[1] // N_HEADS_MEM
    dims_nt = (((1,), (1,)), ((), ()))
    for h in range(N_HEADS_MEM):
        sl = slice(h * dh, (h + 1) * dh)
        s = lax.dot_general(q_ref[:, sl], kbf_ref[:, sl], dims_nt, preferred_element_type=F32) * (dh ** -0.5)
        e = jnp.exp(s - jnp.max(s, axis=1, keepdims=True))
        p = e / jnp.sum(e, axis=1, keepdims=True)
        o_ref[:, sl] = jnp.dot(p.astype(BF16), vbf_ref[:, sl], preferred_element_type=F32).astype(o_ref.dtype)


def mem_attn_prompt(qm, kv, batch, seq, n_mem, tq=512):
    d = qm.shape[1]
    nq = seq // tq
    return pl.pallas_call(
        _mem_attn_prompt_kernel, grid=(batch, nq),
        in_specs=[pl.BlockSpec((tq, d), lambda b, i: (b * nq + i, 0)),
                  pl.BlockSpec((n_mem, d), lambda b, i: (b, 0)),
                  pl.BlockSpec((n_mem, d), lambda b, i: (b, 1))],
        out_specs=pl.BlockSpec((tq, d), lambda b, i: (b * nq + i, 0)),
        out_shape=jax.ShapeDtypeStruct((batch * seq, d), BF16),
        scratch_shapes=[pltpu.VMEM((n_mem, d), BF16), pltpu.VMEM((n_mem, d), BF16)],
        compiler_params=_cparams("parallel", "arbitrary"), name="mem_attn_prompt",
    )(qm, kv, kv)


def _mem_attn_decode_kernel(q_ref, k_ref, v_ref, o_ref):
    for s in range(q_ref.shape[0]):
        q = q_ref[s]
        sc = jnp.sum(k_ref[s] * q[None], axis=-1, keepdims=True) * (q.shape[-1] ** -0.5)
        e = jnp.exp(sc - jnp.max(sc, axis=0, keepdims=True))
        p = e / jnp.sum(e, axis=0, keepdims=True)
        o_ref[s] = jnp.sum(p * v_ref[s], axis=0)


def mem_attn_decode(q, cache_mem_k, cache_mem_v, layer, bb=2):
    db = q.shape[0]
    assert db % bb == 0
    blk = cache_mem_k.shape[2:]
    tok = pl.BlockSpec((bb,) + q.shape[1:], lambda b: (b, 0, 0))
    kv = pl.BlockSpec((None, bb) + blk, lambda b: (layer, b, 0, 0, 0))
    return pl.pallas_call(
        _mem_attn_decode_kernel, grid=(db // bb,), in_specs=[tok, kv, kv], out_specs=tok,
        out_shape=jax.ShapeDtypeStruct(q.shape, F32),
        compiler_params=_cparams("parallel"), name="mem_attn_decode",
    )(q, cache_mem_k, cache_mem_v)


def _router_kernel(x_ref, w_ref, idx_ref, gate_ref):
    logits = jnp.dot(x_ref[...], w_ref[...], preferred_element_type=F32, precision=lax.Precision.HIGHEST)
    n_exp = logits.shape[1]
    lane = lax.broadcasted_iota(jnp.int32, logits.shape, 1)
    m1 = jnp.max(logits, axis=1, keepdims=True)
    i1 = jnp.min(jnp.where(logits == m1, lane, n_exp), axis=1, keepdims=True)
    rest = jnp.where(lane == i1, -jnp.inf, logits)
    m2 = jnp.max(rest, axis=1, keepdims=True)
    i2 = jnp.min(jnp.where(rest == m2, lane, n_exp), axis=1, keepdims=True)
    e2 = jnp.exp(m2 - m1)
    idx_ref[:, 0:1] = i1
    idx_ref[:, 1:2] = i2
    gate_ref[:, 0:1] = 1.0 / (1.0 + e2)
    gate_ref[:, 1:2] = e2 / (1.0 + e2)


def router_top2(x, w_router, bm):
    m, d = x.shape
    n_exp = w_router.shape[-1]
    row = lambda w: pl.BlockSpec((bm, w), lambda i: (i, 0))
    return pl.pallas_call(
        _router_kernel, grid=(m // bm,),
        in_specs=[row(d), pl.BlockSpec((d, n_exp), lambda i: (0, 0))],
        out_specs=[row(TOP_K), row(TOP_K)],
        out_shape=[jax.ShapeDtypeStruct((m, TOP_K), jnp.int32), jax.ShapeDtypeStruct((m, TOP_K), F32)],
        compiler_params=_cparams("parallel"), name="router",
    )(x, w_router)


def _moe_plan(idx, n_exp, bm, n_tiles):
    e_flat = idx.reshape(-1)
    onehot = (e_flat[:, None] == jnp.arange(n_exp, dtype=jnp.int32)[None, :]).astype(jnp.int32)
    csum = jnp.cumsum(onehot, axis=0)
    counts = csum[-1]
    padded = ((counts + bm - 1) // bm) * bm
    gend = jnp.cumsum(padded)
    gstart = gend - padded
    dest = jnp.sum(onehot * (gstart[None, :] + csum - 1), axis=1).astype(jnp.int32)
    tile_start = jnp.arange(n_tiles, dtype=jnp.int32) * bm
    tile_expert = jnp.sum((tile_start[:, None] >= gend[None, :]).astype(jnp.int32), axis=1)
    tile_expert = jnp.minimum(tile_expert, n_exp - 1).astype(jnp.int32)
    n_valid = (gend[-1] // bm).astype(jnp.int32).reshape(1)
    ids = jnp.arange(n_exp, dtype=jnp.int32)
    later = jnp.logical_and(ids[None, :] > ids[:, None], (counts > 0)[None, :])
    first = jnp.min(jnp.where(counts > 0, ids, n_exp))
    nxt = jnp.min(jnp.where(later, ids[None, :], n_exp), axis=1)
    wrap = (nxt == n_exp).astype(jnp.int32)
    nxt = jnp.where(nxt == n_exp, first, nxt).astype(jnp.int32)
    return dest, (tile_expert, nxt[tile_expert], wrap[tile_expert], n_valid)


def _row_copy(src_ref, src_row, dst_ref, dst_row, sem):
    return pltpu.make_async_copy(src_ref.at[pl.ds(src_row, 1), :], dst_ref.at[pl.ds(dst_row, 1), :], sem)


def _moe_scatter_kernel(bm, dest_ref, x_ref, xs_in_ref, xs_ref, sem):
    del xs_in_ref
    base = pl.program_id(0) * bm * TOP_K

    def start(r, _):
        for k in range(TOP_K):
            _row_copy(x_ref, r, xs_ref, dest_ref[base + r * TOP_K + k], sem).start()
        return 0

    def wait(r, _):
        for k in range(TOP_K):
            _row_copy(x_ref, 0, xs_ref, 0, sem).wait()
        return 0

    lax.fori_loop(0, bm, start, 0)
    lax.fori_loop(0, bm, wait, 0)


def moe_scatter(x, dest, n_rows, bm):
    m, d = x.shape
    zeros = jnp.zeros((n_rows, d), x.dtype)
    return pl.pallas_call(
        functools.partial(_moe_scatter_kernel, bm),
        grid_spec=pltpu.PrefetchScalarGridSpec(
            num_scalar_prefetch=1, grid=(m // bm,),
            in_specs=[pl.BlockSpec((bm, d), lambda i, dest: (i, 0)), pl.BlockSpec(memory_space=pl.ANY)],
            out_specs=pl.BlockSpec(memory_space=pl.ANY),
            scratch_shapes=[pltpu.SemaphoreType.DMA(())]),
        out_shape=jax.ShapeDtypeStruct((n_rows, d), x.dtype),
        input_output_aliases={2: 0},
        compiler_params=_cparams("arbitrary"), name="moe_scatter",
    )(dest, x, zeros)


def _expert_changed(te_ref, i):
    return jnp.logical_or(i == 0, te_ref[i] != te_ref[jnp.maximum(i - 1, 0)])


def _weight_block_copies(w_hbm_refs, stage_ref, sem, e, j, bn):
    col = pl.multiple_of(j * bn, bn)
    return [pltpu.make_async_copy(w.at[0, e, :, pl.ds(col, bn)], stage_ref.at[k], sem.at[k])
            for k, w in enumerate(w_hbm_refs)]


def _gmm_weights_step(te_ref, nx_ref, wrap_ref, nv_ref, w_hbm_refs, stage_ref, wbf_ref, sem, bn):
    j, i = pl.program_id(0), pl.program_id(1)

    @pl.when(jnp.logical_and(j == 0, i == 0))
    def _():
        for c in _weight_block_copies(w_hbm_refs, stage_ref, sem, te_ref[0], 0, bn):
            c.start()

    @pl.when(jnp.logical_and(i < nv_ref[0], _expert_changed(te_ref, i)))
    def _():
        for c in _weight_block_copies(w_hbm_refs, stage_ref, sem, te_ref[i], j, bn):
            c.wait()
        wbf_ref[...] = stage_ref[...].astype(BF16)
        nj = j + wrap_ref[i]

        @pl.when(nj < pl.num_programs(0))
        def _():
            for c in _weight_block_copies(w_hbm_refs, stage_ref, sem, nx_ref[i], nj, bn):
                c.start()


def _gmm_up_kernel(bn, te_ref, nx_ref, wrap_ref, nv_ref, x_ref, wg_ref, wu_ref, o_ref, stage_ref, wbf_ref, sem):
    i = pl.program_id(1)
    _gmm_weights_step(te_ref, nx_ref, wrap_ref, nv_ref, (wg_ref, wu_ref), stage_ref, wbf_ref, sem, bn)

    @pl.when(i < nv_ref[0])
    def _():
        x = x_ref[...].astype(BF16)
        g = jnp.dot(x, wbf_ref[0], preferred_element_type=F32)
        u = jnp.dot(x, wbf_ref[1], preferred_element_type=F32)
        o_ref[...] = (g * jax.nn.sigmoid(g) * u).astype(o_ref.dtype)

    @pl.when(i >= nv_ref[0])
    def _():
        o_ref[...] = jnp.zeros_like(o_ref)


def gmm_up(xs, wg, wu, plan, bm, bn):
    rows, k = xs.shape
    f = wg.shape[-1]
    hbm = pl.BlockSpec(memory_space=pl.ANY)
    return pl.pallas_call(
        functools.partial(_gmm_up_kernel, bn),
        grid_spec=pltpu.PrefetchScalarGridSpec(
            num_scalar_prefetch=4, grid=(f // bn, rows // bm),
            in_specs=[pl.BlockSpec((bm, k), lambda j, i, *_: (i, 0)), hbm, hbm],
            out_specs=pl.BlockSpec((bm, bn), lambda j, i, *_: (i, j)),
            scratch_shapes=[pltpu.VMEM((2, k, bn), F32), pltpu.VMEM((2, k, bn), BF16),
                            pltpu.SemaphoreType.DMA((2,))]),
        out_shape=jax.ShapeDtypeStruct((rows, f), BF16),
        compiler_params=_cparams("arbitrary", "arbitrary"), name="gmm_up",
    )(*plan, xs, wg, wu)


def _gmm_down_kernel(bn, te_ref, nx_ref, wrap_ref, nv_ref, a_ref, w_ref, o_ref, stage_ref, wbf_ref, sem):
    i = pl.program_id(1)
    _gmm_weights_step(te_ref, nx_ref, wrap_ref, nv_ref, (w_ref,), stage_ref, wbf_ref, sem, bn)

    @pl.when(i < nv_ref[0])
    def _():
        o_ref[...] = jnp.dot(a_ref[...], wbf_ref[0], preferred_element_type=F32)

    @pl.when(i >= nv_ref[0])
    def _():
        o_ref[...] = jnp.zeros_like(o_ref)


def gmm_down(a, wd, plan, bm, bn):
    rows, f = a.shape
    d = wd.shape[-1]
    return pl.pallas_call(
        functools.partial(_gmm_down_kernel, bn),
        grid_spec=pltpu.PrefetchScalarGridSpec(
            num_scalar_prefetch=4, grid=(d // bn, rows // bm),
            in_specs=[pl.BlockSpec((bm, f), lambda j, i, *_: (i, 0)), pl.BlockSpec(memory_space=pl.ANY)],
            out_specs=pl.BlockSpec((bm, bn), lambda j, i, *_: (i, j)),
            scratch_shapes=[pltpu.VMEM((1, f, bn), F32), pltpu.VMEM((1, f, bn), BF16),
                            pltpu.SemaphoreType.DMA((1,))]),
        out_shape=jax.ShapeDtypeStruct((rows, d), F32),
        compiler_params=_cparams("arbitrary", "arbitrary"), name="gmm_down",
    )(*plan, a, wd)


def _moe_combine_kernel(bm, alpha, dest_ref, x_ref, gate_ref, g_ref, b_ref, ys_ref, o_ref, obf_ref, buf_ref, sem):
    base = pl.program_id(0) * bm * TOP_K

    def start(r, _):
        for k in range(TOP_K):
            _row_copy(ys_ref, dest_ref[base + r * TOP_K + k], buf_ref.at[k], r, sem).start()
        return 0

    def wait(r, _):
        for k in range(TOP_K):
            _row_copy(ys_ref, 0, buf_ref.at[k], 0, sem).wait()
        return 0

    lax.fori_loop(0, bm, start, 0)
    lax.fori_loop(0, bm, wait, 0)
    f = gate_ref[:, 0:1] * buf_ref[0]
    for k in range(1, TOP_K):
        f = f + gate_ref[:, k:k + 1] * buf_ref[k]
    y = _ln_rows(alpha * x_ref[...] + f, g_ref[...], b_ref[...])
    o_ref[...] = y
    obf_ref[...] = y.astype(BF16)


def moe_combine(x, gates, ys, dest, g, b, alpha, bm):
    m, d = x.shape
    row = lambda w: pl.BlockSpec((bm, w), lambda i, dest: (i, 0))
    vec = pl.BlockSpec((1, d), lambda i, dest: (0, 0))
    return pl.pallas_call(
        functools.partial(_moe_combine_kernel, bm, alpha),
        grid_spec=pltpu.PrefetchScalarGridSpec(
            num_scalar_prefetch=1, grid=(m // bm,),
            in_specs=[row(d), row(TOP_K), vec, vec, pl.BlockSpec(memory_space=pl.ANY)],
            out_specs=[row(d), row(d)],
            scratch_shapes=[pltpu.VMEM((TOP_K, bm, d), F32), pltpu.SemaphoreType.DMA(())]),
        out_shape=[jax.ShapeDtypeStruct((m, d), F32), jax.ShapeDtypeStruct((m, d), BF16)],
        compiler_params=_cparams("arbitrary"), name="moe_combine",
    )(dest, x, gates, g.reshape(1, d), b.reshape(1, d), ys)


def moe_ffn_ln(x, w_router, wg, wu, wd, g, b, alpha, moe_layer):
    m, d = x.shape
    n_exp = w_router.shape[-1]
    bm_tok = _row_block(m, 320)
    bm = 512
    n_tiles = -(-(m * TOP_K + n_exp * (bm - 1)) // bm)
    idx, gates = router_top2(x, w_router[moe_layer], bm_tok)
    dest, plan = _moe_plan(idx, n_exp, bm, n_tiles)
    xs = moe_scatter(x, dest, n_tiles * bm, bm_tok)
    a = gmm_up(xs, wg[moe_layer:moe_layer + 1], wu[moe_layer:moe_layer + 1], plan, bm, min(1024, wg.shape[-1]))
    ys = gmm_down(a, wd[moe_layer:moe_layer + 1], plan, bm, 512)
    return moe_combine(x, gates, ys, dest, g, b, alpha, bm_tok)


def kernel(x_prompt, x_sample, mem_prompt, cache_k, cache_v, cache_mem_k, cache_mem_v, state_ssm, state_conv, page_table, ln_in_g, ln_in_b, w_in, conv_w, conv_b, dt_bias, a_log, d_skip, ssd_norm_g, lam_params, subln_g, w_out, w_mem_q, w_mem_kv, w_mem_o, ln_g, ln_b, w_ff_gate, w_ff_up, w_ff_down, w_router, w_exp_gate, w_exp_up, w_exp_down):
    batch, seq, d = x_prompt.shape
    db = x_sample.shape[0]
    assert x_sample.shape[1] == 1
    depth = w_in.shape[0]
    n_mem = mem_prompt.shape[1]
    mp = batch * seq
    m = mp + db
    alpha = (2 * depth) ** 0.25
    d_attn = N_HEADS_A * 128
    d_ssd = N_HEADS_S * 64
    cdim = d_ssd + 2 * SSD_GROUPS * SSD_STATE
    n_main = 3 * d_attn + d_ssd + cdim
    bm = _row_block(m, 1664)
    bm_small = _row_block(m, 416)

    x_all = jnp.concatenate([x_prompt.reshape(mp, d), x_sample.reshape(db, d)], axis=0)
    x, x_bf = layer_norm_in(x_all, ln_in_g, ln_in_b)
    mem2d = mem_prompt.reshape(batch * n_mem, d)

    outs = {k: [] for k in ("sp", "cp", "mkp", "mvp", "ks", "vs", "ss", "cs")}
    u_layers = []
    for l in range(depth):
        lam_init = 0.8 - 0.6 * math.exp(-0.3 * l)
        u = matmul(x_bf, w_in, lead=(l,), ncols=n_main, bm=bm, bn=512, name="mm_in")
        w_dt = jnp.pad(w_in[l][:, n_main:], ((0, 0), (0, 128 - N_HEADS_S)))
        dt_raw = matmul(x_bf, w_dt, bm=bm, bn=128, name="mm_dt")
        o_a = attn_prompt(u, lam_params, subln_g, l, batch, seq, lam_init)
        y_s, ssm_p, conv_p = ssd_prompt(u, dt_raw, conv_w, conv_b, dt_bias, a_log, d_skip, ssd_norm_g, l, batch, seq)
        us = u[mp:]
        heads = lambda a: a.reshape(db, N_HEADS_A, 128)
        o_a_s = attn_decode(heads(us[:, :d_attn]), heads(us[:, d_attn:2 * d_attn]), heads(us[:, 2 * d_attn:3 * d_attn]),
                            cache_k, cache_v, page_table, lam_params, subln_g, l, lam_init)
        y_s_s, ssm_s, conv_s = ssd_decode(us[:, n_main - cdim:n_main], us[:, 3 * d_attn:3 * d_attn + d_ssd], dt_raw[mp:],
                                          state_conv, state_ssm, conv_w, conv_b, dt_bias, a_log, d_skip, ssd_norm_g, l)
        mix = jnp.concatenate([jnp.concatenate([o_a, y_s], axis=1),
                               jnp.concatenate([o_a_s.reshape(db, d_attn), y_s_s.reshape(db, d_ssd)], axis=1).astype(BF16)],
                              axis=0)
        x, x_bf = matmul_add_ln(mix, w_out, x, ln_g[l, 0], ln_b[l, 0], alpha, lead=(l,), bm=bm_small, name="mm_out_ln")
        qm = matmul(x_bf, w_mem_q, lead=(l,), bm=bm, bn=512, out_dtype=BF16, name="mm_mem_q")
        kv = matmul(mem2d, w_mem_kv, lead=(l,), bm=_row_block(batch * n_mem, 512), bn=512, name="mm_mem_kv")
        c_p = mem_attn_prompt(qm, kv, batch, seq, n_mem)
        c_s = mem_attn_decode(qm[mp:].astype(F32).reshape(db, N_HEADS_MEM, d // N_HEADS_MEM), cache_mem_k, cache_mem_v, l)
        c = jnp.concatenate([c_p, c_s.reshape(db, d).astype(BF16)], axis=0)
        x, x_bf = matmul_add_ln(c, w_mem_o, x, ln_g[l, 1], ln_b[l, 1], alpha, lead=(l,), bm=bm_small, name="mm_mem_o_ln")
        if l % 2 == 0:
            act = swiglu_up(x_bf, w_ff_gate, w_ff_up, lead=(l // 2,), bm=bm, bn=512)
            a = matmul(act, w_ff_down, lead=(l // 2,), bm=bm_small, bn=512, name="mm_ff_down")
            x, x_bf = add_layer_norm(x, a, ln_g[l, 2], ln_b[l, 2], alpha)
        else:
            x, x_bf = moe_ffn_ln(x, w_router, w_exp_gate, w_exp_up, w_exp_down, ln_g[l, 2], ln_b[l, 2], alpha, l // 2)
        u_layers.append(u)
        outs["sp"].append(ssm_p.reshape(batch, N_HEADS_S, 64, SSD_STATE))
        outs["cp"].append(conv_p[:, 8 - (CONV_K - 1):, :])
        outs["mkp"].append(kv[:, :d].reshape(batch, n_mem, N_HEADS_MEM, d // N_HEADS_MEM))
        outs["mvp"].append(kv[:, d:].reshape(batch, n_mem, N_HEADS_MEM, d // N_HEADS_MEM))
        outs["ks"].append(us[:, d_attn:2 * d_attn].reshape(db, 1, N_HEADS_A, 128))
        outs["vs"].append(us[:, 2 * d_attn:3 * d_attn].reshape(db, 1, N_HEADS_A, 128))
        outs["ss"].append(ssm_s)
        outs["cs"].append(conv_s)
    st = {k: jnp.stack(v) for k, v in outs.items()}
    kp, vp = kv_prompt_outputs(u_layers, mp)
    st["kp"] = kp.reshape(depth, batch, seq, N_HEADS_A, 128)
    st["vp"] = vp.reshape(depth, batch, seq, N_HEADS_A, 128)
    return (x[:mp].reshape(batch, seq, d), x[mp:].reshape(db, 1, d),
            st["kp"], st["vp"], st["sp"], st["cp"], st["mkp"], st["mvp"], st["ks"], st["vs"], st["ss"], st["cs"])
```

```python
import functools
import math

import jax
import jax.numpy as jnp
from jax import lax
from jax.experimental import pallas as pl
from jax.experimental.pallas import tpu as pltpu

F32 = jnp.float32
BF16 = jnp.bfloat16

LN_EPS = 1e-5
RMS_EPS = 1e-5
N_HEADS_A = 8
QK_DIM = 64
N_HEADS_S = 16
SSD_GROUPS = 2
SSD_STATE = 128
SSD_CHUNK = 128
CONV_K = 4
N_HEADS_MEM = 4
TOP_K = 2
SSD_DECODE_SEQS = 4
VMEM_LIMIT = 56 * 1024 * 1024


def _cparams(*sem):
    return pltpu.CompilerParams(dimension_semantics=sem, vmem_limit_bytes=VMEM_LIMIT)


def _row_block(m, target):
    best = None
    for d in range(16, min(m, target) + 1, 16):
        if m % d == 0:
            best = d
    assert best is not None, (m, target)
    return best


def _ln_rows(x, g, b):
    mu = jnp.mean(x, axis=-1, keepdims=True)
    xc = x - mu
    var = jnp.mean(xc * xc, axis=-1, keepdims=True)
    return xc * lax.rsqrt(var + LN_EPS) * g + b


def _ln_kernel(x_ref, g_ref, b_ref, o_ref, obf_ref):
    y = _ln_rows(x_ref[...], g_ref[...], b_ref[...])
    o_ref[...] = y
    obf_ref[...] = y.astype(BF16)


def layer_norm_in(x, g, b):
    m, d = x.shape
    bm = _row_block(m, 512)
    row = pl.BlockSpec((bm, d), lambda i: (i, 0))
    vec = pl.BlockSpec((1, d), lambda i: (0, 0))
    return pl.pallas_call(
        _ln_kernel, grid=(m // bm,), in_specs=[row, vec, vec], out_specs=[row, row],
        out_shape=[jax.ShapeDtypeStruct((m, d), F32), jax.ShapeDtypeStruct((m, d), BF16)],
        compiler_params=_cparams("parallel"), name="ln_in",
    )(x, g.reshape(1, d), b.reshape(1, d))


def _add_ln_kernel(alpha, x_ref, a_ref, g_ref, b_ref, o_ref, obf_ref):
    y = _ln_rows(alpha * x_ref[...] + a_ref[...], g_ref[...], b_ref[...])
    o_ref[...] = y
    obf_ref[...] = y.astype(BF16)


def add_layer_norm(x, a, g, b, alpha):
    m, d = x.shape
    bm = _row_block(m, 512)
    row = pl.BlockSpec((bm, d), lambda i: (i, 0))
    vec = pl.BlockSpec((1, d), lambda i: (0, 0))
    return pl.pallas_call(
        functools.partial(_add_ln_kernel, alpha), grid=(m // bm,),
        in_specs=[row, row, vec, vec], out_specs=[row, row],
        out_shape=[jax.ShapeDtypeStruct((m, d), F32), jax.ShapeDtypeStruct((m, d), BF16)],
        compiler_params=_cparams("parallel"), name="add_ln",
    )(x, a, g.reshape(1, d), b.reshape(1, d))


def _mm_kernel(w_is_nk, x_ref, w_ref, o_ref, wbf_ref):
    @pl.when(pl.program_id(1) == 0)
    def _():
        wbf_ref[...] = w_ref[...].astype(BF16)

    dims = (((1,), (1 if w_is_nk else 0,)), ((), ()))
    o_ref[...] = lax.dot_general(x_ref[...].astype(BF16), wbf_ref[...], dims,
                                 preferred_element_type=F32).astype(o_ref.dtype)


def matmul(x, w, *, lead=(), col0=0, ncols=None, bm, bn, out_dtype=F32, w_is_nk=False, name="mm"):
    m, k = x.shape
    n_axis, k_axis = (-2, -1) if w_is_nk else (-1, -2)
    n = w.shape[n_axis] - col0 if ncols is None else ncols
    assert w.shape[k_axis] == k and m % bm == 0 and col0 % bn == 0
    nl = len(lead)
    cb0 = col0 // bn
    if w_is_nk:
        w_spec = pl.BlockSpec((None,) * nl + (bn, k), lambda j, i: tuple(lead) + (cb0 + j, 0))
    else:
        w_spec = pl.BlockSpec((None,) * nl + (k, bn), lambda j, i: tuple(lead) + (0, cb0 + j))
    return pl.pallas_call(
        functools.partial(_mm_kernel, w_is_nk), grid=(pl.cdiv(n, bn), m // bm),
        in_specs=[pl.BlockSpec((bm, k), lambda j, i: (i, 0)), w_spec],
        out_specs=pl.BlockSpec((bm, bn), lambda j, i: (i, j)),
        out_shape=jax.ShapeDtypeStruct((m, n), out_dtype),
        scratch_shapes=[pltpu.VMEM((bn, k) if w_is_nk else (k, bn), BF16)],
        compiler_params=_cparams("parallel", "arbitrary"), name=name,
    )(x, w)


def _mm_add_ln_kernel(alpha, x_ref, w_ref, r_ref, g_ref, b_ref, o_ref, obf_ref, wbf_ref):
    @pl.when(pl.program_id(0) == 0)
    def _():
        wbf_ref[...] = w_ref[...].astype(BF16)

    a = jnp.dot(x_ref[...], wbf_ref[...], preferred_element_type=F32)
    y = _ln_rows(alpha * r_ref[...] + a, g_ref[...], b_ref[...])
    o_ref[...] = y
    obf_ref[...] = y.astype(BF16)


def matmul_add_ln(x, w, resid, g, b, alpha, *, lead, bm, name):
    m, k = x.shape
    n = w.shape[-1]
    assert m % bm == 0 and resid.shape == (m, n)
    row = lambda width: pl.BlockSpec((bm, width), lambda i: (i, 0))
    vec = pl.BlockSpec((1, n), lambda i: (0, 0))
    w_spec = pl.BlockSpec((None,) * len(lead) + (k, n), lambda i: tuple(lead) + (0, 0), pipeline_mode=pl.Buffered(1))
    return pl.pallas_call(
        functools.partial(_mm_add_ln_kernel, alpha), grid=(m // bm,),
        in_specs=[row(k), w_spec, row(n), vec, vec], out_specs=[row(n), row(n)],
        out_shape=[jax.ShapeDtypeStruct((m, n), F32), jax.ShapeDtypeStruct((m, n), BF16)],
        scratch_shapes=[pltpu.VMEM((k, n), BF16)],
        compiler_params=_cparams("arbitrary"), name=name,
    )(x, w, resid, g.reshape(1, n), b.reshape(1, n))


def _swiglu_up_kernel(x_ref, wg_ref, wu_ref, o_ref, wgbf_ref, wubf_ref):
    @pl.when(pl.program_id(1) == 0)
    def _():
        wgbf_ref[...] = wg_ref[...].astype(BF16)
        wubf_ref[...] = wu_ref[...].astype(BF16)

    x = x_ref[...]
    g = jnp.dot(x, wgbf_ref[...], preferred_element_type=F32)
    u = jnp.dot(x, wubf_ref[...], preferred_element_type=F32)
    o_ref[...] = (g * jax.nn.sigmoid(g) * u).astype(o_ref.dtype)


def swiglu_up(x, wg, wu, *, lead, bm, bn):
    m, k = x.shape
    n = wg.shape[-1]
    assert m % bm == 0 and n % bn == 0
    nl = len(lead)
    w_spec = pl.BlockSpec((None,) * nl + (k, bn), lambda j, i: tuple(lead) + (0, j))
    return pl.pallas_call(
        _swiglu_up_kernel, grid=(n // bn, m // bm),
        in_specs=[pl.BlockSpec((bm, k), lambda j, i: (i, 0)), w_spec, w_spec],
        out_specs=pl.BlockSpec((bm, bn), lambda j, i: (i, j)),
        out_shape=jax.ShapeDtypeStruct((m, n), BF16),
        scratch_shapes=[pltpu.VMEM((k, bn), BF16), pltpu.VMEM((k, bn), BF16)],
        compiler_params=_cparams("parallel", "arbitrary"), name="swiglu_up",
    )(x, wg, wu)


def _lambda_value(lp, lam_init):
    t1 = jnp.sum(lp[0:1, :] * lp[1:2, :], axis=1, keepdims=True)
    t2 = jnp.sum(lp[2:3, :] * lp[3:4, :], axis=1, keepdims=True)
    return jnp.exp(t1) - jnp.exp(t2) + lam_init


LOG2E = 1.4426950408889634


def _attn_prompt_kernel(tq, lam_init, q_ref, k_ref, v_ref, lp_ref, g_ref, mix_ref, o_ref, kbf_ref, vt_ref):
    del mix_ref
    h = pl.program_id(1)
    qi = pl.program_id(2)

    @pl.when(qi == 0)
    def _():
        kbf_ref[...] = k_ref[...].astype(BF16)
        for c in range(vt_ref.shape[0]):
            vt_ref[c] = v_ref[c * tq:(c + 1) * tq, :].T.astype(BF16)

    slope = jnp.exp2(-(h + 1).astype(F32) * jnp.ones((1, 1), F32)) * LOG2E
    lam = _lambda_value(lp_ref[...], lam_init)
    qt = (q_ref[...] * (QK_DIM ** -0.5 * LOG2E)).T
    sub = lax.broadcasted_iota(jnp.int32, qt.shape, 0)
    qts = (jnp.where(sub < QK_DIM, qt, 0.0).astype(BF16), jnp.where(sub >= QK_DIM, qt, 0.0).astype(BF16))
    krow = lax.broadcasted_iota(jnp.int32, (tq, tq), 0)
    qcol = lax.broadcasted_iota(jnp.int32, (tq, tq), 1)
    base = -slope * (qcol - krow).astype(F32)

    def chunk(kj, carry, masked):
        kc = kbf_ref[pl.ds(pl.multiple_of(kj * tq, tq), tq), :]
        vtc = vt_ref[kj]
        off = -slope * ((qi - kj) * tq).astype(F32)
        out = []
        for qm, (m, l, acc) in zip(qts, carry):
            t = jnp.dot(kc, qm, preferred_element_type=F32) + base
            if masked:
                t = jnp.where(krow <= qcol, t, -jnp.inf)
            m_new = jnp.maximum(m, jnp.max(t, axis=0, keepdims=True) + off)
            p = jnp.exp2(t - (m_new - off))
            alpha = jnp.exp2(m - m_new)
            l_new = alpha * l + jnp.sum(p, axis=0, keepdims=True)
            acc_new = alpha * acc + jnp.dot(vtc, p.astype(BF16), preferred_element_type=F32)
            out.append((m_new, l_new, acc_new))
        return tuple(out)

    init_one = (jnp.full((1, tq), -1e30, F32), jnp.zeros((1, tq), F32), jnp.zeros((128, tq), F32))
    carry = lax.fori_loop(0, qi, lambda kj, c: chunk(kj, c, False), (init_one, init_one))
    (_, l1, a1), (_, l2, a2) = chunk(qi, carry, True)
    o = (a1 / l1 - lam * (a2 / l2)).T
    o = o * lax.rsqrt(jnp.mean(o * o, axis=1, keepdims=True) + RMS_EPS)
    o_ref[...] = (o * g_ref[...] * (1.0 - lam_init)).astype(o_ref.dtype)


def attn_prompt(u, lam_params, subln_g, layer, batch, seq, lam_init, mix, tq=512):
    nq = seq // tq
    lp_spec = pl.BlockSpec((None, 4, QK_DIM), lambda b, h, i: (layer, 0, 0))
    g_spec = pl.BlockSpec((None, 1, 128), lambda b, h, i: (layer, 0, 0))
    return pl.pallas_call(
        functools.partial(_attn_prompt_kernel, tq, lam_init),
        grid=(batch, N_HEADS_A, nq),
        in_specs=[pl.BlockSpec((tq, 128), lambda b, h, i: (b * nq + i, h)),
                  pl.BlockSpec((seq, 128), lambda b, h, i: (b, N_HEADS_A + h)),
                  pl.BlockSpec((seq, 128), lambda b, h, i: (b, 2 * N_HEADS_A + h)),
                  lp_spec, g_spec, pl.BlockSpec(memory_space=pl.ANY)],
        out_specs=pl.BlockSpec((tq, 128), lambda b, h, i: (b * nq + i, h)),
        out_shape=jax.ShapeDtypeStruct(mix.shape, mix.dtype),
        input_output_aliases={5: 0},
        scratch_shapes=[pltpu.VMEM((seq, 128), BF16), pltpu.VMEM((nq, 128, tq), BF16)],
        compiler_params=_cparams("parallel", "parallel", "arbitrary"), name="attn_prompt",
    )(u, u, u, lam_params, subln_g.reshape(-1, 1, 128), mix)


def _softplus(x):
    return jnp.maximum(x, 0.0) + jnp.log1p(jnp.exp(-jnp.abs(x)))


def _silu(x):
    return x * jax.nn.sigmoid(x)


def _ssd_prompt_kernel(xs_ref, bc_ref, z_ref, dt_ref, cw_ref, cb_ref, dtb_ref, alog_ref, dsk_ref, ng_ref, mix_ref,
                       y_ref, st_ref, conv_ref, xp_ref, h_ref):
    del mix_ref
    c = pl.program_id(1)
    nc = pl.num_programs(1)
    q = SSD_CHUNK
    d_ssd = N_HEADS_S * 64

    @pl.when(c == 0)
    def _():
        xp_ref[0:8, :] = jnp.zeros((8, xp_ref.shape[1]), F32)
        h_ref[...] = jnp.zeros_like(h_ref)

    xp_ref[8:8 + q, 0:d_ssd] = xs_ref[...]
    xp_ref[8:8 + q, d_ssd:] = bc_ref[...]
    cw = cw_ref[...]
    xc = cb_ref[...] + cw[3:4, :] * xp_ref[8:8 + q, :]
    for j in range(1, CONV_K):
        xc = xc + cw[3 - j:4 - j, :] * xp_ref[8 - j:8 - j + q, :]
    xp_ref[0:8, :] = xp_ref[q:q + 8, :]
    xc = _silu(xc)
    xs = xc[:, :d_ssd]

    head_lane = lax.broadcasted_iota(jnp.int32, (q, 128), 1) < N_HEADS_S
    dt_raw = jnp.where(head_lane, dt_ref[...], 0.0)
    dt = _softplus(dt_raw + dtb_ref[...])
    a_neg = -jnp.exp(alog_ref[...])
    da = dt * a_neg
    ri = lax.broadcasted_iota(jnp.int32, (q, q), 0)
    ci = lax.broadcasted_iota(jnp.int32, (q, q), 1)
    causal = ci <= ri
    tril = jnp.where(causal, 1.0, 0.0).astype(F32)
    a_cs = jnp.dot(tril, da, preferred_element_type=F32, precision=lax.Precision.HIGHEST)
    a_cs_t = a_cs.T
    a_last = a_cs[q - 1:q, :]
    e_cs = jnp.exp(a_cs)
    e_end = jnp.exp(a_last - a_cs)
    e_last = jnp.exp(a_last)
    lane = lax.broadcasted_iota(jnp.int32, (q, 128), 1)
    lo = lane < 64
    rsel = lax.broadcasted_iota(jnp.int32, (128, SSD_STATE), 0) < 64
    dims_nt = (((1,), (1,)), ((), ()))
    dims_tn = (((0,), (0,)), ((), ()))
    hpg = N_HEADS_S // SSD_GROUPS

    ys = []
    for g in range(SSD_GROUPS):
        bm_g = xc[:, d_ssd + g * SSD_STATE:d_ssd + (g + 1) * SSD_STATE].astype(BF16)
        cm_g = xc[:, d_ssd + (SSD_GROUPS + g) * SSD_STATE:d_ssd + (SSD_GROUPS + g + 1) * SSD_STATE].astype(BF16)
        cb = lax.dot_general(cm_g, bm_g, dims_nt, preferred_element_type=F32)
        for pr in range(hpg // 2):
            h0 = g * hpg + 2 * pr
            x_pair = xs[:, h0 * 64:h0 * 64 + 128]
            dt_pair = jnp.where(lo, dt[:, h0:h0 + 1], dt[:, h0 + 1:h0 + 2])
            xdt = x_pair * dt_pair
            y_pair = jnp.zeros((q, 128), F32)
            for k, keep in ((0, lo), (1, jnp.logical_not(lo))):
                hh = h0 + k
                seg = a_cs[:, hh:hh + 1] - a_cs_t[hh:hh + 1, :]
                decay = jnp.exp(jnp.where(causal, seg, -jnp.inf))
                mat = (cb * decay).astype(BF16)
                y_pair = y_pair + jnp.dot(mat, jnp.where(keep, xdt, 0.0).astype(BF16), preferred_element_type=F32)
            end_pair = jnp.where(lo, e_end[:, h0:h0 + 1], e_end[:, h0 + 1:h0 + 2])
            cs_pair = jnp.where(lo, e_cs[:, h0:h0 + 1], e_cs[:, h0 + 1:h0 + 2])
            h_prev = h_ref[h0 * 64:h0 * 64 + 128, :]
            y_off = lax.dot_general(cm_g, h_prev.astype(BF16), dims_nt, preferred_element_type=F32) * cs_pair
            st = lax.dot_general((xdt * end_pair).astype(BF16), bm_g, dims_tn, preferred_element_type=F32)
            dec = jnp.where(rsel, e_last[:, h0:h0 + 1], e_last[:, h0 + 1:h0 + 2])
            h_ref[h0 * 64:h0 * 64 + 128, :] = dec * h_prev + st
            dsk_pair = jnp.where(lo[0:1, :], dsk_ref[:, h0:h0 + 1], dsk_ref[:, h0 + 1:h0 + 2])
            ys.append(y_pair + y_off + dsk_pair * x_pair)
    y = jnp.concatenate(ys, axis=1)
    gz = y * _silu(z_ref[...])
    half = d_ssd // SSD_GROUPS
    outs = []
    for g in range(SSD_GROUPS):
        part = gz[:, g * half:(g + 1) * half]
        outs.append(part * lax.rsqrt(jnp.mean(part * part, axis=1, keepdims=True) + RMS_EPS))
    y_ref[...] = (jnp.concatenate(outs, axis=1) * ng_ref[...]).astype(y_ref.dtype)

    @pl.when(c == nc - 1)
    def _():
        st_ref[...] = h_ref[...]
        conv_ref[:, 0:d_ssd] = xs_ref[q - 8:q, :]
        conv_ref[:, d_ssd:] = bc_ref[q - 8:q, :]


def _pad_lanes(v):
    return jnp.pad(v.reshape(1, -1), ((0, 0), (0, 128 - v.shape[-1])))


def ssd_prompt(u, conv_w, conv_b, dt_bias, a_log, d_skip, norm_g, layer, batch, seq, mix):
    q = SSD_CHUNK
    nc = seq // q
    d_ssd = N_HEADS_S * 64
    d_bc = 2 * SSD_GROUPS * SSD_STATE
    cdim = d_ssd + d_bc
    d_attn = N_HEADS_A * 128
    z0, x0, bc0 = 3 * d_attn, 3 * d_attn + d_ssd, 3 * d_attn + 2 * d_ssd
    dt0 = bc0 + d_bc
    assert z0 % d_ssd == 0 and x0 % d_ssd == 0 and bc0 % d_bc == 0 and dt0 % 128 == 0
    assert u.shape[1] == dt0 + N_HEADS_S
    row = lambda b, c: (b * nc + c, 0)
    vec = lambda width: pl.BlockSpec((1, width), lambda b, c: (0, 0))
    return pl.pallas_call(
        _ssd_prompt_kernel, grid=(batch, nc),
        in_specs=[pl.BlockSpec((q, d_ssd), lambda b, c: (b * nc + c, x0 // d_ssd)),
                  pl.BlockSpec((q, d_bc), lambda b, c: (b * nc + c, bc0 // d_bc)),
                  pl.BlockSpec((q, d_ssd), lambda b, c: (b * nc + c, z0 // d_ssd)),
                  pl.BlockSpec((q, 128), lambda b, c: (b * nc + c, dt0 // 128)),
                  pl.BlockSpec((None, CONV_K, cdim), lambda b, c: (layer, 0, 0)),
                  vec(cdim), vec(128), vec(128), vec(128), vec(d_ssd), pl.BlockSpec(memory_space=pl.ANY)],
        out_specs=[pl.BlockSpec((q, d_ssd), lambda b, c: (b * nc + c, d_attn // d_ssd)),
                   pl.BlockSpec((None, N_HEADS_S * 64, SSD_STATE), lambda b, c: (b, 0, 0)),
                   pl.BlockSpec((None, 8, cdim), lambda b, c: (b, 0, 0))],
        out_shape=[jax.ShapeDtypeStruct(mix.shape, mix.dtype),
                   jax.ShapeDtypeStruct((batch, N_HEADS_S * 64, SSD_STATE), F32),
                   jax.ShapeDtypeStruct((batch, 8, cdim), F32)],
        input_output_aliases={10: 0},
        scratch_shapes=[pltpu.VMEM((q + 8, cdim), F32), pltpu.VMEM((N_HEADS_S * 64, SSD_STATE), F32)],
        compiler_params=_cparams("parallel", "arbitrary"), name="ssd_prompt",
    )(u, u, u, u, conv_w, conv_b[layer].reshape(1, -1), _pad_lanes(dt_bias[layer]),
      _pad_lanes(a_log[layer]), _pad_lanes(d_skip[layer]), norm_g[layer].reshape(1, -1), mix)


def _kv_heads_kernel(k0_ref, v0_ref, k1_ref, v1_ref, ko_ref, vo_ref):
    def emit(k_ref, v_ref):
        for h in range(N_HEADS_A):
            ko_ref[:, h, :] = k_ref[:, h * 128:(h + 1) * 128]
            vo_ref[:, h, :] = v_ref[:, h * 128:(h + 1) * 128]

    @pl.when(pl.program_id(0) == 0)
    def _():
        emit(k0_ref, v0_ref)

    @pl.when(pl.program_id(0) == 1)
    def _():
        emit(k1_ref, v1_ref)


def kv_prompt_outputs(us, rows, tl=512):
    assert len(us) == 2 and rows % tl == 0
    nb = rows // tl
    d_attn = N_HEADS_A * 128

    def col(which, layer):
        idle = nb - 1 if layer == 0 else 0
        return pl.BlockSpec((tl, d_attn), lambda l, i: (jnp.where(l == layer, i, idle), which))

    out = pl.BlockSpec((None, tl, N_HEADS_A, 128), lambda l, i: (l, i, 0, 0))
    shape = jax.ShapeDtypeStruct((2, rows, N_HEADS_A, 128), F32)
    return pl.pallas_call(
        _kv_heads_kernel, grid=(2, nb),
        in_specs=[col(1, 0), col(2, 0), col(1, 1), col(2, 1)], out_specs=[out, out], out_shape=[shape, shape],
        compiler_params=_cparams("arbitrary", "arbitrary"), name="kv_heads",
    )(us[0], us[0], us[1], us[1])


def _attn_decode_kernel(n_pages, page, lam_init, pt_ref, q_ref, kn_ref, vn_ref, lp_ref, g_ref, *refs):
    k_refs, v_refs = refs[:n_pages], refs[n_pages:2 * n_pages]
    o_ref, s_ref = refs[2 * n_pages], refs[2 * n_pages + 1]
    lam = _lambda_value(lp_ref[...], lam_init)
    q = q_ref[...] * (QK_DIM ** -0.5 * LOG2E)
    lo = lax.broadcasted_iota(jnp.int32, q.shape, 1) < QK_DIM
    slope = jnp.exp2(-(lax.broadcasted_iota(jnp.int32, q.shape, 0) + 1).astype(F32)) * LOG2E
    tok_bias = slope[None] * lax.broadcasted_iota(jnp.int32, (page, 1, 1), 0).astype(F32)
    past = n_pages * page

    ri = lax.broadcasted_iota(jnp.int32, (128, 128), 0) < QK_DIM
    ci = lax.broadcasted_iota(jnp.int32, (128, 128), 1) < QK_DIM
    half_sum = jnp.where(ri == ci, 1.0, 0.0).astype(BF16)

    def packed_scores(k):
        prod = (k * q).reshape(-1, 128).astype(BF16)
        return jnp.dot(prod, half_sum, preferred_element_type=F32).reshape(k.shape)

    s_new = packed_scores(kn_ref[...])
    m = s_new
    page_bias = [slope * float(past - p * page) for p in range(n_pages)]
    for p in range(n_pages):
        s = packed_scores(k_refs[p][...]) + tok_bias
        s_ref[p * page:(p + 1) * page] = s
        m = jnp.maximum(m, jnp.max(s, axis=0) - page_bias[p])
    e_new = jnp.exp2(s_new - m)
    l = e_new
    for p in range(n_pages):
        e = jnp.exp2(s_ref[p * page:(p + 1) * page] - (m + page_bias[p])[None])
        s_ref[p * page:(p + 1) * page] = e
        l = l + jnp.sum(e, axis=0)
    r = 1.0 / l
    coef = jnp.where(lo, r, -lam * r)
    w_new = e_new * coef
    acc = (w_new + pltpu.roll(w_new, QK_DIM, 1)) * vn_ref[...]
    for p in range(n_pages):
        w = s_ref[p * page:(p + 1) * page] * coef[None]
        w = w + pltpu.roll(w, QK_DIM, 2)
        acc = acc + jnp.sum(w * v_refs[p][...], axis=0)
    o = acc * lax.rsqrt(jnp.mean(acc * acc, axis=-1, keepdims=True) + RMS_EPS)
    o_ref[...] = o * g_ref[...] * (1.0 - lam_init)


def attn_decode(q, k_new, v_new, cache_k, cache_v, page_table, lam_params, subln_g, layer, lam_init):
    db, n_pages = page_table.shape
    page = cache_k.shape[2]

    def kv_spec(j):
        return pl.BlockSpec((None, None, page, N_HEADS_A, 128), lambda b, pt: (layer, pt[b, j], 0, 0, 0))

    tok = pl.BlockSpec((None, N_HEADS_A, 128), lambda b, pt: (b, 0, 0))
    in_specs = [tok, tok, tok,
                pl.BlockSpec((None, 4, QK_DIM), lambda b, pt: (layer, 0, 0)),
                pl.BlockSpec((None, 1, 128), lambda b, pt: (layer, 0, 0))]
    in_specs += [kv_spec(j) for j in range(n_pages)] * 2
    return pl.pallas_call(
        functools.partial(_attn_decode_kernel, n_pages, page, lam_init),
        grid_spec=pltpu.PrefetchScalarGridSpec(
            num_scalar_prefetch=1, grid=(db,), in_specs=in_specs, out_specs=tok,
            scratch_shapes=[pltpu.VMEM((n_pages * page, N_HEADS_A, 128), F32)]),
        out_shape=jax.ShapeDtypeStruct((db, N_HEADS_A, 128), F32),
        compiler_params=_cparams("parallel"), name="attn_decode",
    )(page_table, q, k_new, v_new, lam_params, subln_g.reshape(-1, 1, 128), *([cache_k] * n_pages),
      *([cache_v] * n_pages))


def _ssd_decode_kernel(layer, first, *refs):
    (xbc_ref, z_ref, dt_ref, cst_ref, ssm_ref, cw_ref, cb_ref, dtb_ref, alog_ref, dsk_ref, ng_ref) = refs[:11]
    y_ref, ssm_out_ref, conv_out_ref = refs[-3:]
    if first:
        for other in range(ssm_out_ref.shape[0]):
            if other != layer:
                ssm_out_ref[other] = jnp.zeros(ssm_out_ref.shape[1:], F32)
        ssm_out_ref = ssm_out_ref.at[layer]
    for s in range(xbc_ref.shape[0]):
        _ssd_decode_one(xbc_ref.at[s], z_ref.at[s], dt_ref.at[s], cst_ref.at[s], ssm_ref.at[s], cw_ref, cb_ref, dtb_ref,
                        alog_ref, dsk_ref, ng_ref, y_ref.at[s], ssm_out_ref.at[s], conv_out_ref.at[s])


def _ssd_decode_one(xbc_ref, z_ref, dt_ref, cst_ref, ssm_ref, cw_ref, cb_ref, dtb_ref, alog_ref, dsk_ref,
                    ng_ref, y_ref, ssm_out_ref, conv_out_ref):
    d_ssd = N_HEADS_S * 64
    hpg = N_HEADS_S // SSD_GROUPS
    xnew = xbc_ref[...]
    cst = cst_ref[...]
    cw = cw_ref[...]
    xc = cb_ref[...] + cw[3:4, :] * xnew
    for j in range(CONV_K - 1):
        xc = xc + cw[j:j + 1, :] * cst[j:j + 1, :]
    xc = _silu(xc)
    conv_out_ref[0:2, :] = cst[1:3, :]
    conv_out_ref[2:3, :] = xnew

    dt = _softplus(dt_ref[...] + dtb_ref[...])
    dec = jnp.exp(dt * (-jnp.exp(alog_ref[...])))
    eye = lax.broadcasted_iota(jnp.int32, (128, 128), 0) == lax.broadcasted_iota(jnp.int32, (128, 128), 1)
    lo = lax.broadcasted_iota(jnp.int32, (1, 128), 1) < 64
    dims_nt = (((1,), (1,)), ((), ()))
    ys = []
    for pr in range(N_HEADS_S // 2):
        h0 = 2 * pr
        g = h0 // hpg
        x_pair = xc[:, h0 * 64:h0 * 64 + 128]
        b_g = xc[:, d_ssd + g * SSD_STATE:d_ssd + (g + 1) * SSD_STATE]
        c_g = xc[:, d_ssd + (SSD_GROUPS + g) * SSD_STATE:d_ssd + (SSD_GROUPS + g + 1) * SSD_STATE]
        dt_pair = jnp.where(lo, dt[:, h0:h0 + 1], dt[:, h0 + 1:h0 + 2])
        xdt_row = x_pair * dt_pair
        xdt_diag = jnp.where(eye, jnp.broadcast_to(xdt_row, (128, 128)), 0.0).astype(BF16)
        upd = jnp.dot(xdt_diag, jnp.broadcast_to(b_g, (128, SSD_STATE)).astype(BF16), preferred_element_type=F32)
        h_prev = ssm_ref[h0:h0 + 2]
        h_new = jnp.concatenate([jnp.broadcast_to(dec[:, h0 + k:h0 + k + 1], (64, SSD_STATE)) * h_prev[k]
                                 for k in range(2)], axis=0) + upd
        ssm_out_ref[h0:h0 + 2] = h_new.reshape(2, 64, SSD_STATE)
        y_rows = lax.dot_general(jnp.broadcast_to(c_g, (8, SSD_STATE)).astype(BF16), h_new.astype(BF16), dims_nt,
                                 preferred_element_type=F32)
        dsk_pair = jnp.where(lo, dsk_ref[:, h0:h0 + 1], dsk_ref[:, h0 + 1:h0 + 2])
        ys.append(y_rows[0:1, :] + dsk_pair * x_pair)
    y = jnp.concatenate(ys, axis=1)
    gz = y * _silu(z_ref[...])
    half = d_ssd // SSD_GROUPS
    outs = []
    for g in range(SSD_GROUPS):
        part = gz[:, g * half:(g + 1) * half]
        outs.append(part * lax.rsqrt(jnp.mean(part * part, axis=1, keepdims=True) + RMS_EPS))
    y_ref[...] = jnp.concatenate(outs, axis=1) * ng_ref[...]


def ssd_decode(xbc, z, dt_raw, state_conv, state_ssm, conv_w, conv_b, dt_bias, a_log, d_skip, norm_g, layer,
               ssm_stack=None):
    db, cdim = xbc.shape
    depth = state_ssm.shape[0]
    d_ssd = N_HEADS_S * 64
    bb = SSD_DECODE_SEQS
    assert db % bb == 0
    one = lambda width: pl.BlockSpec((bb, 1, width), lambda b: (b, 0, 0))
    vec = lambda width: pl.BlockSpec((1, width), lambda b: (0, 0))
    ssm_shape = state_ssm.shape[2:]
    first = ssm_stack is None
    in_specs = [one(cdim), one(d_ssd), one(128),
                pl.BlockSpec((None, bb, CONV_K - 1, cdim), lambda b: (layer, b, 0, 0)),
                pl.BlockSpec((None, bb) + ssm_shape, lambda b: (layer, b, 0, 0, 0)),
                pl.BlockSpec((None, CONV_K, cdim), lambda b: (layer, 0, 0)),
                vec(cdim), vec(128), vec(128), vec(128), vec(d_ssd)]
    args = [xbc.reshape(db, 1, cdim), z.reshape(db, 1, d_ssd), dt_raw.reshape(db, 1, 128), state_conv, state_ssm,
            conv_w, conv_b[layer].reshape(1, -1), _pad_lanes(dt_bias[layer]), _pad_lanes(a_log[layer]),
            _pad_lanes(d_skip[layer]), norm_g[layer].reshape(1, -1)]
    if first:
        ssm_out = pl.BlockSpec((depth, bb) + ssm_shape, lambda b: (0, b, 0, 0, 0))
        aliases = {}
    else:
        ssm_out = pl.BlockSpec((None, bb) + ssm_shape, lambda b: (layer, b, 0, 0, 0))
        in_specs.append(pl.BlockSpec(memory_space=pl.ANY))
        args.append(ssm_stack)
        aliases = {len(args) - 1: 1}
    return pl.pallas_call(
        functools.partial(_ssd_decode_kernel, layer, first), grid=(db // bb,),
        in_specs=in_specs,
        out_specs=[one(d_ssd), ssm_out, pl.BlockSpec((bb, CONV_K - 1, cdim), lambda b: (b, 0, 0))],
        out_shape=[jax.ShapeDtypeStruct((db, 1, d_ssd), F32),
                   jax.ShapeDtypeStruct((depth, db) + ssm_shape, F32),
                   jax.ShapeDtypeStruct((db, CONV_K - 1, cdim), F32)],
        input_output_aliases=aliases,
        compiler_params=_cparams("arbitrary"), name="ssd_decode",
    )(*args)


def _mem_attn_prompt_kernel(q_ref, k_ref, v_ref, o_ref, kbf_ref, vbf_ref):
    @pl.when(pl.program_id(1) == 0)
    def _():
        kbf_ref[...] = k_ref[...].astype(BF16)
        vbf_ref[...] = v_ref[...].astype(BF16)

    dh = q_ref.shape[1] // N_HEADS_MEM
    dims_nt = (((1,), (1,)), ((), ()))
    for h in range(N_HEADS_MEM):
        sl = slice(h * dh, (h + 1) * dh)
        s = lax.dot_general(q_ref[:, sl], kbf_ref[:, sl], dims_nt, preferred_element_type=F32) * (dh ** -0.5)
        e = jnp.exp(s - jnp.max(s, axis=1, keepdims=True))
        p = e / jnp.sum(e, axis=1, keepdims=True)
        o_ref[:, sl] = jnp.dot(p.astype(BF16), vbf_ref[:, sl], preferred_element_type=F32).astype(o_ref.dtype)


def mem_attn_prompt(qm, kv, batch, seq, n_mem, tq=512):
    d = qm.shape[1]
    nq = seq // tq
    return pl.pallas_call(
        _mem_attn_prompt_kernel, grid=(batch, nq),
        in_specs=[pl.BlockSpec((tq, d), lambda b, i: (b * nq + i, 0)),
                  pl.BlockSpec((n_mem, d), lambda b, i: (b, 0)),
                  pl.BlockSpec((n_mem, d), lambda b, i: (b, 1))],
        out_specs=pl.BlockSpec((tq, d), lambda b, i: (b * nq + i, 0)),
        out_shape=jax.ShapeDtypeStruct((batch * seq, d), BF16),
        scratch_shapes=[pltpu.VMEM((n_mem, d), BF16), pltpu.VMEM((n_mem, d), BF16)],
        compiler_params=_cparams("parallel", "arbitrary"), name="mem_attn_prompt",
    )(qm, kv, kv)


def _mem_attn_decode_kernel(q_ref, k_ref, v_ref, o_ref):
    for s in range(q_ref.shape[0]):
        q = q_ref[s]
        sc = jnp.sum(k_ref[s] * q[None], axis=-1, keepdims=True) * (q.shape[-1] ** -0.5)
        e = jnp.exp(sc - jnp.max(sc, axis=0, keepdims=True))
        p = e / jnp.sum(e, axis=0, keepdims=True)
        o_ref[s] = jnp.sum(p * v_ref[s], axis=0)


def mem_attn_decode(q, cache_mem_k, cache_mem_v, layer, bb=2):
    db = q.shape[0]
    assert db % bb == 0
    blk = cache_mem_k.shape[2:]
    tok = pl.BlockSpec((bb,) + q.shape[1:], lambda b: (b, 0, 0))
    kv = pl.BlockSpec((None, bb) + blk, lambda b: (layer, b, 0, 0, 0))
    return pl.pallas_call(
        _mem_attn_decode_kernel, grid=(db // bb,), in_specs=[tok, kv, kv], out_specs=tok,
        out_shape=jax.ShapeDtypeStruct(q.shape, F32),
        compiler_params=_cparams("parallel"), name="mem_attn_decode",
    )(q, cache_mem_k, cache_mem_v)


def _router_kernel(x_ref, w_ref, idx_ref, gate_ref):
    logits = jnp.dot(x_ref[...], w_ref[...], preferred_element_type=F32, precision=lax.Precision.HIGHEST)
    n_exp = logits.shape[1]
    lane = lax.broadcasted_iota(jnp.int32, logits.shape, 1)
    m1 = jnp.max(logits, axis=1, keepdims=True)
    i1 = jnp.min(jnp.where(logits == m1, lane, n_exp), axis=1, keepdims=True)
    rest = jnp.where(lane == i1, -jnp.inf, logits)
    m2 = jnp.max(rest, axis=1, keepdims=True)
    i2 = jnp.min(jnp.where(rest == m2, lane, n_exp), axis=1, keepdims=True)
    e2 = jnp.exp(m2 - m1)
    idx_ref[:, 0:1] = i1
    idx_ref[:, 1:2] = i2
    gate_ref[:, 0:1] = 1.0 / (1.0 + e2)
    gate_ref[:, 1:2] = e2 / (1.0 + e2)


def router_top2(x, w_router, bm):
    m, d = x.shape
    n_exp = w_router.shape[-1]
    row = lambda w: pl.BlockSpec((bm, w), lambda i: (i, 0))
    return pl.pallas_call(
        _router_kernel, grid=(m // bm,),
        in_specs=[row(d), pl.BlockSpec((d, n_exp), lambda i: (0, 0))],
        out_specs=[row(TOP_K), row(TOP_K)],
        out_shape=[jax.ShapeDtypeStruct((m, TOP_K), jnp.int32), jax.ShapeDtypeStruct((m, TOP_K), F32)],
        compiler_params=_cparams("parallel"), name="router",
    )(x, w_router)


def _moe_plan(idx, n_exp, bm, n_tiles):
    e_flat = idx.reshape(-1)
    onehot = (e_flat[:, None] == jnp.arange(n_exp, dtype=jnp.int32)[None, :]).astype(jnp.int32)
    csum = jnp.cumsum(onehot, axis=0)
    counts = csum[-1]
    padded = ((counts + bm - 1) // bm) * bm
    gend = jnp.cumsum(padded)
    gstart = gend - padded
    dest = jnp.sum(onehot * (gstart[None, :] + csum - 1), axis=1).astype(jnp.int32)
    tile_start = jnp.arange(n_tiles, dtype=jnp.int32) * bm
    tile_expert = jnp.sum((tile_start[:, None] >= gend[None, :]).astype(jnp.int32), axis=1)
    tile_expert = jnp.minimum(tile_expert, n_exp - 1).astype(jnp.int32)
    n_valid = (gend[-1] // bm).astype(jnp.int32).reshape(1)
    ids = jnp.arange(n_exp, dtype=jnp.int32)
    later = jnp.logical_and(ids[None, :] > ids[:, None], (counts > 0)[None, :])
    first = jnp.min(jnp.where(counts > 0, ids, n_exp))
    nxt = jnp.min(jnp.where(later, ids[None, :], n_exp), axis=1)
    wrap = (nxt == n_exp).astype(jnp.int32)
    nxt = jnp.where(nxt == n_exp, first, nxt).astype(jnp.int32)
    return dest, (tile_expert, nxt[tile_expert], wrap[tile_expert], n_valid)


def _row_copy(src_ref, src_row, dst_ref, dst_row, sem):
    return pltpu.make_async_copy(src_ref.at[pl.ds(src_row, 1), :], dst_ref.at[pl.ds(dst_row, 1), :], sem)


def _moe_scatter_kernel(bm, dest_ref, x_ref, xs_in_ref, xs_ref, sem):
    del xs_in_ref
    base = pl.program_id(0) * bm * TOP_K

    def start(r, _):
        for k in range(TOP_K):
            _row_copy(x_ref, r, xs_ref, dest_ref[base + r * TOP_K + k], sem).start()
        return 0

    def wait(r, _):
        for k in range(TOP_K):
            _row_copy(x_ref, 0, xs_ref, 0, sem).wait()
        return 0

    lax.fori_loop(0, bm, start, 0)
    lax.fori_loop(0, bm, wait, 0)


def moe_scatter(x, dest, n_rows, bm):
    m, d = x.shape
    zeros = jnp.zeros((n_rows, d), x.dtype)
    return pl.pallas_call(
        functools.partial(_moe_scatter_kernel, bm),
        grid_spec=pltpu.PrefetchScalarGridSpec(
            num_scalar_prefetch=1, grid=(m // bm,),
            in_specs=[pl.BlockSpec((bm, d), lambda i, dest: (i, 0)), pl.BlockSpec(memory_space=pl.ANY)],
            out_specs=pl.BlockSpec(memory_space=pl.ANY),
            scratch_shapes=[pltpu.SemaphoreType.DMA(())]),
        out_shape=jax.ShapeDtypeStruct((n_rows, d), x.dtype),
        input_output_aliases={2: 0},
        compiler_params=_cparams("arbitrary"), name="moe_scatter",
    )(dest, x, zeros)


def _expert_changed(te_ref, i):
    return jnp.logical_or(i == 0, te_ref[i] != te_ref[jnp.maximum(i - 1, 0)])


def _weight_block_copies(w_hbm_refs, stage_ref, sem, e, j, bn):
    col = pl.multiple_of(j * bn, bn)
    return [pltpu.make_async_copy(w.at[0, e, :, pl.ds(col, bn)], stage_ref.at[k], sem.at[k])
            for k, w in enumerate(w_hbm_refs)]


def _gmm_weights_step(te_ref, nx_ref, wrap_ref, nv_ref, w_hbm_refs, stage_ref, wbf_ref, sem, bn):
    j, i = pl.program_id(0), pl.program_id(1)

    @pl.when(jnp.logical_and(j == 0, i == 0))
    def _():
        for c in _weight_block_copies(w_hbm_refs, stage_ref, sem, te_ref[0], 0, bn):
            c.start()

    @pl.when(jnp.logical_and(i < nv_ref[0], _expert_changed(te_ref, i)))
    def _():
        for c in _weight_block_copies(w_hbm_refs, stage_ref, sem, te_ref[i], j, bn):
            c.wait()
        wbf_ref[...] = stage_ref[...].astype(BF16)
        nj = j + wrap_ref[i]

        @pl.when(nj < pl.num_programs(0))
        def _():
            for c in _weight_block_copies(w_hbm_refs, stage_ref, sem, nx_ref[i], nj, bn):
                c.start()


def _gmm_up_kernel(bn, te_ref, nx_ref, wrap_ref, nv_ref, x_ref, wg_ref, wu_ref, o_ref, stage_ref, wbf_ref, sem):
    i = pl.program_id(1)
    _gmm_weights_step(te_ref, nx_ref, wrap_ref, nv_ref, (wg_ref, wu_ref), stage_ref, wbf_ref, sem, bn)

    @pl.when(i < nv_ref[0])
    def _():
        x = x_ref[...].astype(BF16)
        g = jnp.dot(x, wbf_ref[0], preferred_element_type=F32)
        u = jnp.dot(x, wbf_ref[1], preferred_element_type=F32)
        o_ref[...] = (g * jax.nn.sigmoid(g) * u).astype(o_ref.dtype)

    @pl.when(i >= nv_ref[0])
    def _():
        o_ref[...] = jnp.zeros_like(o_ref)


def gmm_up(xs, wg, wu, plan, bm, bn):
    rows, k = xs.shape
    f = wg.shape[-1]
    hbm = pl.BlockSpec(memory_space=pl.ANY)
    return pl.pallas_call(
        functools.partial(_gmm_up_kernel, bn),
        grid_spec=pltpu.PrefetchScalarGridSpec(
            num_scalar_prefetch=4, grid=(f // bn, rows // bm),
            in_specs=[pl.BlockSpec((bm, k), lambda j, i, *_: (i, 0)), hbm, hbm],
            out_specs=pl.BlockSpec((bm, bn), lambda j, i, *_: (i, j)),
            scratch_shapes=[pltpu.VMEM((2, k, bn), F32), pltpu.VMEM((2, k, bn), BF16),
                            pltpu.SemaphoreType.DMA((2,))]),
        out_shape=jax.ShapeDtypeStruct((rows, f), BF16),
        compiler_params=_cparams("arbitrary", "arbitrary"), name="gmm_up",
    )(*plan, xs, wg, wu)


def _gmm_down_kernel(bn, te_ref, nx_ref, wrap_ref, nv_ref, a_ref, w_ref, o_ref, stage_ref, wbf_ref, sem):
    i = pl.program_id(1)
    _gmm_weights_step(te_ref, nx_ref, wrap_ref, nv_ref, (w_ref,), stage_ref, wbf_ref, sem, bn)

    @pl.when(i < nv_ref[0])
    def _():
        o_ref[...] = jnp.dot(a_ref[...], wbf_ref[0], preferred_element_type=F32)

    @pl.when(i >= nv_ref[0])
    def _():
        o_ref[...] = jnp.zeros_like(o_ref)


def gmm_down(a, wd, plan, bm, bn):
    rows, f = a.shape
    d = wd.shape[-1]
    return pl.pallas_call(
        functools.partial(_gmm_down_kernel, bn),
        grid_spec=pltpu.PrefetchScalarGridSpec(
            num_scalar_prefetch=4, grid=(d // bn, rows // bm),
            in_specs=[pl.BlockSpec((bm, f), lambda j, i, *_: (i, 0)), pl.BlockSpec(memory_space=pl.ANY)],
            out_specs=pl.BlockSpec((bm, bn), lambda j, i, *_: (i, j)),
            scratch_shapes=[pltpu.VMEM((1, f, bn), F32), pltpu.VMEM((1, f, bn), BF16),
                            pltpu.SemaphoreType.DMA((1,))]),
        out_shape=jax.ShapeDtypeStruct((rows, d), F32),
        compiler_params=_cparams("arbitrary", "arbitrary"), name="gmm_down",
    )(*plan, a, wd)


def _moe_combine_kernel(bm, alpha, dest_ref, x_ref, gate_ref, g_ref, b_ref, ys_ref, o_ref, obf_ref, buf_ref, sem):
    base = pl.program_id(0) * bm * TOP_K

    def start(r, _):
        for k in range(TOP_K):
            _row_copy(ys_ref, dest_ref[base + r * TOP_K + k], buf_ref.at[k], r, sem).start()
        return 0

    def wait(r, _):
        for k in range(TOP_K):
            _row_copy(ys_ref, 0, buf_ref.at[k], 0, sem).wait()
        return 0

    lax.fori_loop(0, bm, start, 0)
    lax.fori_loop(0, bm, wait, 0)
    f = gate_ref[:, 0:1] * buf_ref[0]
    for k in range(1, TOP_K):
        f = f + gate_ref[:, k:k + 1] * buf_ref[k]
    y = _ln_rows(alpha * x_ref[...] + f, g_ref[...], b_ref[...])
    o_ref[...] = y
    obf_ref[...] = y.astype(BF16)


def moe_combine(x, gates, ys, dest, g, b, alpha, bm):
    m, d = x.shape
    row = lambda w: pl.BlockSpec((bm, w), lambda i, dest: (i, 0))
    vec = pl.BlockSpec((1, d), lambda i, dest: (0, 0))
    return pl.pallas_call(
        functools.partial(_moe_combine_kernel, bm, alpha),
        grid_spec=pltpu.PrefetchScalarGridSpec(
            num_scalar_prefetch=1, grid=(m // bm,),
            in_specs=[row(d), row(TOP_K), vec, vec, pl.BlockSpec(memory_space=pl.ANY)],
            out_specs=[row(d), row(d)],
            scratch_shapes=[pltpu.VMEM((TOP_K, bm, d), F32), pltpu.SemaphoreType.DMA(())]),
        out_shape=[jax.ShapeDtypeStruct((m, d), F32), jax.ShapeDtypeStruct((m, d), BF16)],
        compiler_params=_cparams("arbitrary"), name="moe_combine",
    )(dest, x, gates, g.reshape(1, d), b.reshape(1, d), ys)


def moe_ffn_ln(x, w_router, wg, wu, wd, g, b, alpha, moe_layer):
    m, d = x.shape
    n_exp = w_router.shape[-1]
    bm_tok = _row_block(m, 320)
    bm = 512
    n_tiles = -(-(m * TOP_K + n_exp * (bm - 1)) // bm)
    idx, gates = router_top2(x, w_router[moe_layer], bm_tok)
    dest, plan = _moe_plan(idx, n_exp, bm, n_tiles)
    xs = moe_scatter(x, dest, n_tiles * bm, bm_tok)
    a = gmm_up(xs, wg[moe_layer:moe_layer + 1], wu[moe_layer:moe_layer + 1], plan, bm, min(1024, wg.shape[-1]))
    ys = gmm_down(a, wd[moe_layer:moe_layer + 1], plan, bm, 512)
    return moe_combine(x, gates, ys, dest, g, b, alpha, bm_tok)


def kernel(x_prompt, x_sample, mem_prompt, cache_k, cache_v, cache_mem_k, cache_mem_v, state_ssm, state_conv, page_table, ln_in_g, ln_in_b, w_in, conv_w, conv_b, dt_bias, a_log, d_skip, ssd_norm_g, lam_params, subln_g, w_out, w_mem_q, w_mem_kv, w_mem_o, ln_g, ln_b, w_ff_gate, w_ff_up, w_ff_down, w_router, w_exp_gate, w_exp_up, w_exp_down):
    batch, seq, d = x_prompt.shape
    db = x_sample.shape[0]
    assert x_sample.shape[1] == 1
    depth = w_in.shape[0]
    n_mem = mem_prompt.shape[1]
    mp = batch * seq
    m = mp + db
    alpha = (2 * depth) ** 0.25
    d_attn = N_HEADS_A * 128
    d_ssd = N_HEADS_S * 64
    cdim = d_ssd + 2 * SSD_GROUPS * SSD_STATE
    n_main = 3 * d_attn + d_ssd + cdim
    bm = _row_block(m, 1664)
    bm_small = _row_block(m, 416)

    x_all = jnp.concatenate([x_prompt.reshape(mp, d), x_sample.reshape(db, d)], axis=0)
    x, x_bf = layer_norm_in(x_all, ln_in_g, ln_in_b)
    mem2d = mem_prompt.reshape(batch * n_mem, d)

    w_in_nk = jnp.swapaxes(w_in, 1, 2)
    outs = {k: [] for k in ("sp", "cp", "mkp", "mvp", "ks", "vs", "cs")}
    ssm_s = None
    u_layers = []
    for l in range(depth):
        lam_init = 0.8 - 0.6 * math.exp(-0.3 * l)
        u = matmul(x_bf, w_in_nk, lead=(l,), bm=bm, bn=512, w_is_nk=True, name="mm_in")
        us = u[mp:]
        heads = lambda a: a.reshape(db, N_HEADS_A, 128)
        o_a_s = attn_decode(heads(us[:, :d_attn]), heads(us[:, d_attn:2 * d_attn]), heads(us[:, 2 * d_attn:3 * d_attn]),
                            cache_k, cache_v, page_table, lam_params, subln_g, l, lam_init)
        dt_s = jnp.pad(us[:, n_main:], ((0, 0), (0, 128 - N_HEADS_S)))
        y_s_s, ssm_s, conv_s = ssd_decode(us[:, n_main - cdim:n_main], us[:, 3 * d_attn:3 * d_attn + d_ssd], dt_s,
                                          state_conv, state_ssm, conv_w, conv_b, dt_bias, a_log, d_skip, ssd_norm_g, l, ssm_s)
        mix_s = jnp.concatenate([o_a_s.reshape(db, d_attn), y_s_s.reshape(db, d_ssd)], axis=1).astype(BF16)
        mix = jnp.pad(mix_s, ((mp, 0), (0, 0)))
        mix = attn_prompt(u, lam_params, subln_g, l, batch, seq, lam_init, mix)
        mix, ssm_p, conv_p = ssd_prompt(u, conv_w, conv_b, dt_bias, a_log, d_skip, ssd_norm_g, l, batch, seq, mix)
        x, x_bf = matmul_add_ln(mix, w_out, x, ln_g[l, 0], ln_b[l, 0], alpha, lead=(l,), bm=bm_small, name="mm_out_ln")
        qm = matmul(x_bf, w_mem_q, lead=(l,), bm=bm, bn=512, out_dtype=BF16, name="mm_mem_q")
        kv = matmul(mem2d, w_mem_kv, lead=(l,), bm=_row_block(batch * n_mem, 512), bn=512, name="mm_mem_kv")
        c_p = mem_attn_prompt(qm, kv, batch, seq, n_mem)
        c_s = mem_attn_decode(qm[mp:].astype(F32).reshape(db, N_HEADS_MEM, d // N_HEADS_MEM), cache_mem_k, cache_mem_v, l)
        c = jnp.concatenate([c_p, c_s.reshape(db, d).astype(BF16)], axis=0)
        x, x_bf = matmul_add_ln(c, w_mem_o, x, ln_g[l, 1], ln_b[l, 1], alpha, lead=(l,), bm=bm_small, name="mm_mem_o_ln")
        if l % 2 == 0:
            act = swiglu_up(x_bf, w_ff_gate, w_ff_up, lead=(l // 2,), bm=bm, bn=512)
            a = matmul(act, w_ff_down, lead=(l // 2,), bm=bm_small, bn=512, name="mm_ff_down")
            x, x_bf = add_layer_norm(x, a, ln_g[l, 2], ln_b[l, 2], alpha)
        else:
            x, x_bf = moe_ffn_ln(x, w_router, w_exp_gate, w_exp_up, w_exp_down, ln_g[l, 2], ln_b[l, 2], alpha, l // 2)
        u_layers.append(u)
        outs["sp"].append(ssm_p.reshape(batch, N_HEADS_S, 64, SSD_STATE))
        outs["cp"].append(conv_p[:, 8 - (CONV_K - 1):, :])
        outs["mkp"].append(kv[:, :d].reshape(batch, n_mem, N_HEADS_MEM, d // N_HEADS_MEM))
        outs["mvp"].append(kv[:, d:].reshape(batch, n_mem, N_HEADS_MEM, d // N_HEADS_MEM))
        outs["ks"].append(us[:, d_attn:2 * d_attn].reshape(db, 1, N_HEADS_A, 128))
        outs["vs"].append(us[:, 2 * d_attn:3 * d_attn].reshape(db, 1, N_HEADS_A, 128))
        outs["cs"].append(conv_s)
    st = {k: jnp.stack(v) for k, v in outs.items()}
    kp, vp = kv_prompt_outputs(u_layers, mp)
    st["ss"] = ssm_s
    st["kp"] = kp.reshape(depth, batch, seq, N_HEADS_A, 128)
    st["vp"] = vp.reshape(depth, batch, seq, N_HEADS_A, 128)
    return (x[:mp].reshape(batch, seq, d), x[mp:].reshape(db, 1, d),
            st["kp"], st["vp"], st["sp"], st["cp"], st["mkp"], st["mvp"], st["ks"], st["vs"], st["ss"], st["cs"])
```

```python
import functools
import math

import jax
import jax.numpy as jnp
from jax import lax
from jax.experimental import pallas as pl
from jax.experimental.pallas import tpu as pltpu

F32 = jnp.float32
BF16 = jnp.bfloat16

LN_EPS = 1e-5
RMS_EPS = 1e-5
N_HEADS_A = 8
QK_DIM = 64
N_HEADS_S = 16
SSD_GROUPS = 2
SSD_STATE = 128
SSD_CHUNK = 128
CONV_K = 4
N_HEADS_MEM = 4
TOP_K = 2
SSD_DECODE_SEQS = 4
VMEM_LIMIT = 56 * 1024 * 1024


def _cparams(*sem):
    return pltpu.CompilerParams(dimension_semantics=sem, vmem_limit_bytes=VMEM_LIMIT)


def _row_block(m, target):
    best = None
    for d in range(16, min(m, target) + 1, 16):
        if m % d == 0:
            best = d
    assert best is not None, (m, target)
    return best


def _ln_rows(x, g, b):
    mu = jnp.mean(x, axis=-1, keepdims=True)
    xc = x - mu
    var = jnp.mean(xc * xc, axis=-1, keepdims=True)
    return xc * lax.rsqrt(var + LN_EPS) * g + b


def _ln_kernel(x_ref, g_ref, b_ref, o_ref, obf_ref):
    y = _ln_rows(x_ref[...], g_ref[...], b_ref[...])
    o_ref[...] = y
    obf_ref[...] = y.astype(BF16)


def layer_norm_in(x, g, b):
    m, d = x.shape
    bm = _row_block(m, 512)
    row = pl.BlockSpec((bm, d), lambda i: (i, 0))
    vec = pl.BlockSpec((1, d), lambda i: (0, 0))
    return pl.pallas_call(
        _ln_kernel, grid=(m // bm,), in_specs=[row, vec, vec], out_specs=[row, row],
        out_shape=[jax.ShapeDtypeStruct((m, d), F32), jax.ShapeDtypeStruct((m, d), BF16)],
        compiler_params=_cparams("parallel"), name="ln_in",
    )(x, g.reshape(1, d), b.reshape(1, d))


def _add_ln_kernel(alpha, x_ref, a_ref, g_ref, b_ref, o_ref, obf_ref):
    y = _ln_rows(alpha * x_ref[...] + a_ref[...], g_ref[...], b_ref[...])
    o_ref[...] = y
    obf_ref[...] = y.astype(BF16)


def add_layer_norm(x, a, g, b, alpha):
    m, d = x.shape
    bm = _row_block(m, 512)
    row = pl.BlockSpec((bm, d), lambda i: (i, 0))
    vec = pl.BlockSpec((1, d), lambda i: (0, 0))
    return pl.pallas_call(
        functools.partial(_add_ln_kernel, alpha), grid=(m // bm,),
        in_specs=[row, row, vec, vec], out_specs=[row, row],
        out_shape=[jax.ShapeDtypeStruct((m, d), F32), jax.ShapeDtypeStruct((m, d), BF16)],
        compiler_params=_cparams("parallel"), name="add_ln",
    )(x, a, g.reshape(1, d), b.reshape(1, d))


def _rows_below_zeros_kernel(x_ref, o_ref):
    last = pl.num_programs(0) - 1
    n = x_ref.shape[0]

    @pl.when(pl.program_id(0) != last)
    def _():
        o_ref[...] = jnp.zeros_like(o_ref)

    @pl.when(pl.program_id(0) == last)
    def _():
        o_ref[0:o_ref.shape[0] - n, :] = jnp.zeros((o_ref.shape[0] - n, o_ref.shape[1]), o_ref.dtype)
        o_ref[o_ref.shape[0] - n:, :] = x_ref[...]


def rows_below_zeros(x, m):
    n, d = x.shape
    half = m // 2
    assert m % 2 == 0 and half % 16 == 0 and n % 16 == 0 and n <= half
    return pl.pallas_call(
        _rows_below_zeros_kernel, grid=(2,),
        in_specs=[pl.BlockSpec((n, d), lambda i: (0, 0))],
        out_specs=pl.BlockSpec((half, d), lambda i: (i, 0)),
        out_shape=jax.ShapeDtypeStruct((m, d), x.dtype),
        compiler_params=_cparams("arbitrary"), name="rows_below_zeros",
    )(x)


def _mm_kernel(w_is_nk, x_ref, w_ref, o_ref, wbf_ref):
    @pl.when(pl.program_id(1) == 0)
    def _():
        wbf_ref[...] = w_ref[...].astype(BF16)

    dims = (((1,), (1 if w_is_nk else 0,)), ((), ()))
    o_ref[...] = lax.dot_general(x_ref[...].astype(BF16), wbf_ref[...], dims,
                                 preferred_element_type=F32).astype(o_ref.dtype)


def matmul(x, w, *, lead=(), col0=0, ncols=None, bm, bn, out_dtype=F32, w_is_nk=False, name="mm"):
    m, k = x.shape
    n_axis, k_axis = (-2, -1) if w_is_nk else (-1, -2)
    n = w.shape[n_axis] - col0 if ncols is None else ncols
    assert w.shape[k_axis] == k and m % bm == 0 and col0 % bn == 0
    nl = len(lead)
    cb0 = col0 // bn
    if w_is_nk:
        w_spec = pl.BlockSpec((None,) * nl + (bn, k), lambda j, i: tuple(lead) + (cb0 + j, 0))
    else:
        w_spec = pl.BlockSpec((None,) * nl + (k, bn), lambda j, i: tuple(lead) + (0, cb0 + j))
    return pl.pallas_call(
        functools.partial(_mm_kernel, w_is_nk), grid=(pl.cdiv(n, bn), m // bm),
        in_specs=[pl.BlockSpec((bm, k), lambda j, i: (i, 0)), w_spec],
        out_specs=pl.BlockSpec((bm, bn), lambda j, i: (i, j)),
        out_shape=jax.ShapeDtypeStruct((m, n), out_dtype),
        scratch_shapes=[pltpu.VMEM((bn, k) if w_is_nk else (k, bn), BF16)],
        compiler_params=_cparams("parallel", "arbitrary"), name=name,
    )(x, w)


def _mm_add_ln_kernel(alpha, x_ref, w_ref, r_ref, g_ref, b_ref, o_ref, obf_ref, wbf_ref):
    @pl.when(pl.program_id(0) == 0)
    def _():
        wbf_ref[...] = w_ref[...].astype(BF16)

    a = jnp.dot(x_ref[...], wbf_ref[...], preferred_element_type=F32)
    y = _ln_rows(alpha * r_ref[...] + a, g_ref[...], b_ref[...])
    o_ref[...] = y
    obf_ref[...] = y.astype(BF16)


def matmul_add_ln(x, w, resid, g, b, alpha, *, lead, bm, name):
    m, k = x.shape
    n = w.shape[-1]
    assert m % bm == 0 and resid.shape == (m, n)
    row = lambda width: pl.BlockSpec((bm, width), lambda i: (i, 0))
    vec = pl.BlockSpec((1, n), lambda i: (0, 0))
    w_spec = pl.BlockSpec((None,) * len(lead) + (k, n), lambda i: tuple(lead) + (0, 0), pipeline_mode=pl.Buffered(1))
    return pl.pallas_call(
        functools.partial(_mm_add_ln_kernel, alpha), grid=(m // bm,),
        in_specs=[row(k), w_spec, row(n), vec, vec], out_specs=[row(n), row(n)],
        out_shape=[jax.ShapeDtypeStruct((m, n), F32), jax.ShapeDtypeStruct((m, n), BF16)],
        scratch_shapes=[pltpu.VMEM((k, n), BF16)],
        compiler_params=_cparams("arbitrary"), name=name,
    )(x, w, resid, g.reshape(1, n), b.reshape(1, n))


def _swiglu_up_kernel(x_ref, wg_ref, wu_ref, o_ref, wgbf_ref, wubf_ref):
    @pl.when(pl.program_id(1) == 0)
    def _():
        wgbf_ref[...] = wg_ref[...].astype(BF16)
        wubf_ref[...] = wu_ref[...].astype(BF16)

    x = x_ref[...]
    g = jnp.dot(x, wgbf_ref[...], preferred_element_type=F32)
    u = jnp.dot(x, wubf_ref[...], preferred_element_type=F32)
    o_ref[...] = (g * jax.nn.sigmoid(g) * u).astype(o_ref.dtype)


def swiglu_up(x, wg, wu, *, lead, bm, bn):
    m, k = x.shape
    n = wg.shape[-1]
    assert m % bm == 0 and n % bn == 0
    nl = len(lead)
    w_spec = pl.BlockSpec((None,) * nl + (k, bn), lambda j, i: tuple(lead) + (0, j))
    return pl.pallas_call(
        _swiglu_up_kernel, grid=(n // bn, m // bm),
        in_specs=[pl.BlockSpec((bm, k), lambda j, i: (i, 0)), w_spec, w_spec],
        out_specs=pl.BlockSpec((bm, bn), lambda j, i: (i, j)),
        out_shape=jax.ShapeDtypeStruct((m, n), BF16),
        scratch_shapes=[pltpu.VMEM((k, bn), BF16), pltpu.VMEM((k, bn), BF16)],
        compiler_params=_cparams("parallel", "arbitrary"), name="swiglu_up",
    )(x, wg, wu)


def _lambda_value(lp, lam_init):
    t1 = jnp.sum(lp[0:1, :] * lp[1:2, :], axis=1, keepdims=True)
    t2 = jnp.sum(lp[2:3, :] * lp[3:4, :], axis=1, keepdims=True)
    return jnp.exp(t1) - jnp.exp(t2) + lam_init


LOG2E = 1.4426950408889634


def _attn_prompt_kernel(tq, lam_init, q_ref, k_ref, v_ref, lp_ref, g_ref, mix_ref, o_ref, kbf_ref, vt_ref):
    del mix_ref
    h = pl.program_id(1)
    qi = pl.program_id(2)

    @pl.when(qi == 0)
    def _():
        kbf_ref[...] = k_ref[...].astype(BF16)
        for c in range(vt_ref.shape[0]):
            vt_ref[c] = v_ref[c * tq:(c + 1) * tq, :].T.astype(BF16)

    slope = jnp.exp2(-(h + 1).astype(F32) * jnp.ones((1, 1), F32)) * LOG2E
    lam = _lambda_value(lp_ref[...], lam_init)
    qt = (q_ref[...] * (QK_DIM ** -0.5 * LOG2E)).T
    sub = lax.broadcasted_iota(jnp.int32, qt.shape, 0)
    qts = (jnp.where(sub < QK_DIM, qt, 0.0).astype(BF16), jnp.where(sub >= QK_DIM, qt, 0.0).astype(BF16))
    krow = lax.broadcasted_iota(jnp.int32, (tq, tq), 0)
    qcol = lax.broadcasted_iota(jnp.int32, (tq, tq), 1)
    base = -slope * (qcol - krow).astype(F32)

    def chunk(kj, carry, masked):
        kc = kbf_ref[pl.ds(pl.multiple_of(kj * tq, tq), tq), :]
        vtc = vt_ref[kj]
        off = -slope * ((qi - kj) * tq).astype(F32)
        out = []
        for qm, (m, l, acc) in zip(qts, carry):
            t = jnp.dot(kc, qm, preferred_element_type=F32) + base
            if masked:
                t = jnp.where(krow <= qcol, t, -jnp.inf)
            m_new = jnp.maximum(m, jnp.max(t, axis=0, keepdims=True) + off)
            p = jnp.exp2(t - (m_new - off))
            alpha = jnp.exp2(m - m_new)
            l_new = alpha * l + jnp.sum(p, axis=0, keepdims=True)
            acc_new = alpha * acc + jnp.dot(vtc, p.astype(BF16), preferred_element_type=F32)
            out.append((m_new, l_new, acc_new))
        return tuple(out)

    init_one = (jnp.full((1, tq), -1e30, F32), jnp.zeros((1, tq), F32), jnp.zeros((128, tq), F32))
    carry = lax.fori_loop(0, qi, lambda kj, c: chunk(kj, c, False), (init_one, init_one))
    (_, l1, a1), (_, l2, a2) = chunk(qi, carry, True)
    o = (a1 / l1 - lam * (a2 / l2)).T
    o = o * lax.rsqrt(jnp.mean(o * o, axis=1, keepdims=True) + RMS_EPS)
    o_ref[...] = (o * g_ref[...] * (1.0 - lam_init)).astype(o_ref.dtype)


def attn_prompt(u, lam_params, subln_g, layer, batch, seq, lam_init, mix, tq=512):
    nq = seq // tq
    lp_spec = pl.BlockSpec((None, 4, QK_DIM), lambda b, h, i: (layer, 0, 0))
    g_spec = pl.BlockSpec((None, 1, 128), lambda b, h, i: (layer, 0, 0))
    return pl.pallas_call(
        functools.partial(_attn_prompt_kernel, tq, lam_init),
        grid=(batch, N_HEADS_A, nq),
        in_specs=[pl.BlockSpec((tq, 128), lambda b, h, i: (b * nq + i, h)),
                  pl.BlockSpec((seq, 128), lambda b, h, i: (b, N_HEADS_A + h)),
                  pl.BlockSpec((seq, 128), lambda b, h, i: (b, 2 * N_HEADS_A + h)),
                  lp_spec, g_spec, pl.BlockSpec(memory_space=pl.ANY)],
        out_specs=pl.BlockSpec((tq, 128), lambda b, h, i: (b * nq + i, h)),
        out_shape=jax.ShapeDtypeStruct(mix.shape, mix.dtype),
        input_output_aliases={5: 0},
        scratch_shapes=[pltpu.VMEM((seq, 128), BF16), pltpu.VMEM((nq, 128, tq), BF16)],
        compiler_params=_cparams("parallel", "parallel", "arbitrary"), name="attn_prompt",
    )(u, u, u, lam_params, subln_g.reshape(-1, 1, 128), mix)


def _softplus(x):
    return jnp.maximum(x, 0.0) + jnp.log1p(jnp.exp(-jnp.abs(x)))


def _silu(x):
    return x * jax.nn.sigmoid(x)


def _ssd_prompt_kernel(xs_ref, bc_ref, z_ref, dt_ref, cw_ref, cb_ref, dtb_ref, alog_ref, dsk_ref, ng_ref, mix_ref,
                       y_ref, st_ref, conv_ref, xp_ref, h_ref):
    del mix_ref
    c = pl.program_id(1)
    nc = pl.num_programs(1)
    q = SSD_CHUNK
    d_ssd = N_HEADS_S * 64

    @pl.when(c == 0)
    def _():
        xp_ref[0:8, :] = jnp.zeros((8, xp_ref.shape[1]), F32)
        h_ref[...] = jnp.zeros_like(h_ref)

    xp_ref[8:8 + q, 0:d_ssd] = xs_ref[...]
    xp_ref[8:8 + q, d_ssd:] = bc_ref[...]
    cw = cw_ref[...]
    xc = cb_ref[...] + cw[3:4, :] * xp_ref[8:8 + q, :]
    for j in range(1, CONV_K):
        xc = xc + cw[3 - j:4 - j, :] * xp_ref[8 - j:8 - j + q, :]
    xp_ref[0:8, :] = xp_ref[q:q + 8, :]
    xc = _silu(xc)
    xs = xc[:, :d_ssd]

    head_lane = lax.broadcasted_iota(jnp.int32, (q, 128), 1) < N_HEADS_S
    dt_raw = jnp.where(head_lane, dt_ref[...], 0.0)
    dt = _softplus(dt_raw + dtb_ref[...])
    a_neg = -jnp.exp(alog_ref[...])
    da = dt * a_neg
    ri = lax.broadcasted_iota(jnp.int32, (q, q), 0)
    ci = lax.broadcasted_iota(jnp.int32, (q, q), 1)
    causal = ci <= ri
    tril = jnp.where(causal, 1.0, 0.0).astype(F32)
    a_cs = jnp.dot(tril, da, preferred_element_type=F32, precision=lax.Precision.HIGHEST)
    a_cs_t = a_cs.T
    a_last = a_cs[q - 1:q, :]
    e_cs = jnp.exp(a_cs)
    e_end = jnp.exp(a_last - a_cs)
    e_last = jnp.exp(a_last)
    lane = lax.broadcasted_iota(jnp.int32, (q, 128), 1)
    lo = lane < 64
    rsel = lax.broadcasted_iota(jnp.int32, (128, SSD_STATE), 0) < 64
    dims_nt = (((1,), (1,)), ((), ()))
    dims_tn = (((0,), (0,)), ((), ()))
    hpg = N_HEADS_S // SSD_GROUPS

    ys = []
    for g in range(SSD_GROUPS):
        bm_g = xc[:, d_ssd + g * SSD_STATE:d_ssd + (g + 1) * SSD_STATE].astype(BF16)
        cm_g = xc[:, d_ssd + (SSD_GROUPS + g) * SSD_STATE:d_ssd + (SSD_GROUPS + g + 1) * SSD_STATE].astype(BF16)
        cb = lax.dot_general(cm_g, bm_g, dims_nt, preferred_element_type=F32)
        for pr in range(hpg // 2):
            h0 = g * hpg + 2 * pr
            x_pair = xs[:, h0 * 64:h0 * 64 + 128]
            dt_pair = jnp.where(lo, dt[:, h0:h0 + 1], dt[:, h0 + 1:h0 + 2])
            xdt = x_pair * dt_pair
            y_pair = jnp.zeros((q, 128), F32)
            for k, keep in ((0, lo), (1, jnp.logical_not(lo))):
                hh = h0 + k
                seg = a_cs[:, hh:hh + 1] - a_cs_t[hh:hh + 1, :]
                decay = jnp.exp(jnp.where(causal, seg, -jnp.inf))
                mat = (cb * decay).astype(BF16)
                y_pair = y_pair + jnp.dot(mat, jnp.where(keep, xdt, 0.0).astype(BF16), preferred_element_type=F32)
            end_pair = jnp.where(lo, e_end[:, h0:h0 + 1], e_end[:, h0 + 1:h0 + 2])
            cs_pair = jnp.where(lo, e_cs[:, h0:h0 + 1], e_cs[:, h0 + 1:h0 + 2])
            h_prev = h_ref[h0 * 64:h0 * 64 + 128, :]
            y_off = lax.dot_general(cm_g, h_prev.astype(BF16), dims_nt, preferred_element_type=F32) * cs_pair
            st = lax.dot_general((xdt * end_pair).astype(BF16), bm_g, dims_tn, preferred_element_type=F32)
            dec = jnp.where(rsel, e_last[:, h0:h0 + 1], e_last[:, h0 + 1:h0 + 2])
            h_ref[h0 * 64:h0 * 64 + 128, :] = dec * h_prev + st
            dsk_pair = jnp.where(lo[0:1, :], dsk_ref[:, h0:h0 + 1], dsk_ref[:, h0 + 1:h0 + 2])
            ys.append(y_pair + y_off + dsk_pair * x_pair)
    y = jnp.concatenate(ys, axis=1)
    gz = y * _silu(z_ref[...])
    half = d_ssd // SSD_GROUPS
    outs = []
    for g in range(SSD_GROUPS):
        part = gz[:, g * half:(g + 1) * half]
        outs.append(part * lax.rsqrt(jnp.mean(part * part, axis=1, keepdims=True) + RMS_EPS))
    y_ref[...] = (jnp.concatenate(outs, axis=1) * ng_ref[...]).astype(y_ref.dtype)

    @pl.when(c == nc - 1)
    def _():
        st_ref[...] = h_ref[...]
        conv_ref[:, 0:d_ssd] = xs_ref[q - 8:q, :]
        conv_ref[:, d_ssd:] = bc_ref[q - 8:q, :]


def _pad_lanes(v):
    return jnp.pad(v.reshape(1, -1), ((0, 0), (0, 128 - v.shape[-1])))


def ssd_prompt(u, conv_w, conv_b, dt_bias, a_log, d_skip, norm_g, layer, batch, seq, mix):
    q = SSD_CHUNK
    nc = seq // q
    d_ssd = N_HEADS_S * 64
    d_bc = 2 * SSD_GROUPS * SSD_STATE
    cdim = d_ssd + d_bc
    d_attn = N_HEADS_A * 128
    z0, x0, bc0 = 3 * d_attn, 3 * d_attn + d_ssd, 3 * d_attn + 2 * d_ssd
    dt0 = bc0 + d_bc
    assert z0 % d_ssd == 0 and x0 % d_ssd == 0 and bc0 % d_bc == 0 and dt0 % 128 == 0
    assert u.shape[1] == dt0 + N_HEADS_S
    row = lambda b, c: (b * nc + c, 0)
    vec = lambda width: pl.BlockSpec((1, width), lambda b, c: (0, 0))
    return pl.pallas_call(
        _ssd_prompt_kernel, grid=(batch, nc),
        in_specs=[pl.BlockSpec((q, d_ssd), lambda b, c: (b * nc + c, x0 // d_ssd)),
                  pl.BlockSpec((q, d_bc), lambda b, c: (b * nc + c, bc0 // d_bc)),
                  pl.BlockSpec((q, d_ssd), lambda b, c: (b * nc + c, z0 // d_ssd)),
                  pl.BlockSpec((q, 128), lambda b, c: (b * nc + c, dt0 // 128)),
                  pl.BlockSpec((None, CONV_K, cdim), lambda b, c: (layer, 0, 0)),
                  vec(cdim), vec(128), vec(128), vec(128), vec(d_ssd), pl.BlockSpec(memory_space=pl.ANY)],
        out_specs=[pl.BlockSpec((q, d_ssd), lambda b, c: (b * nc + c, d_attn // d_ssd)),
                   pl.BlockSpec((None, N_HEADS_S * 64, SSD_STATE), lambda b, c: (b, 0, 0)),
                   pl.BlockSpec((None, 8, cdim), lambda b, c: (b, 0, 0))],
        out_shape=[jax.ShapeDtypeStruct(mix.shape, mix.dtype),
                   jax.ShapeDtypeStruct((batch, N_HEADS_S * 64, SSD_STATE), F32),
                   jax.ShapeDtypeStruct((batch, 8, cdim), F32)],
        input_output_aliases={10: 0},
        scratch_shapes=[pltpu.VMEM((q + 8, cdim), F32), pltpu.VMEM((N_HEADS_S * 64, SSD_STATE), F32)],
        compiler_params=_cparams("parallel", "arbitrary"), name="ssd_prompt",
    )(u, u, u, u, conv_w, conv_b[layer].reshape(1, -1), _pad_lanes(dt_bias[layer]),
      _pad_lanes(a_log[layer]), _pad_lanes(d_skip[layer]), norm_g[layer].reshape(1, -1), mix)


def _kv_heads_kernel(k0_ref, v0_ref, k1_ref, v1_ref, ko_ref, vo_ref):
    def emit(k_ref, v_ref):
        for h in range(N_HEADS_A):
            ko_ref[:, h, :] = k_ref[:, h * 128:(h + 1) * 128]
            vo_ref[:, h, :] = v_ref[:, h * 128:(h + 1) * 128]

    @pl.when(pl.program_id(0) == 0)
    def _():
        emit(k0_ref, v0_ref)

    @pl.when(pl.program_id(0) == 1)
    def _():
        emit(k1_ref, v1_ref)


def kv_prompt_outputs(us, rows, tl=512):
    assert len(us) == 2 and rows % tl == 0
    nb = rows // tl
    d_attn = N_HEADS_A * 128

    def col(which, layer):
        idle = nb - 1 if layer == 0 else 0
        return pl.BlockSpec((tl, d_attn), lambda l, i: (jnp.where(l == layer, i, idle), which))

    out = pl.BlockSpec((None, tl, N_HEADS_A, 128), lambda l, i: (l, i, 0, 0))
    shape = jax.ShapeDtypeStruct((2, rows, N_HEADS_A, 128), F32)
    return pl.pallas_call(
        _kv_heads_kernel, grid=(2, nb),
        in_specs=[col(1, 0), col(2, 0), col(1, 1), col(2, 1)], out_specs=[out, out], out_shape=[shape, shape],
        compiler_params=_cparams("arbitrary", "arbitrary"), name="kv_heads",
    )(us[0], us[0], us[1], us[1])


def _attn_decode_kernel(n_pages, page, lam_init, pt_ref, q_ref, kn_ref, vn_ref, lp_ref, g_ref, *refs):
    k_refs, v_refs = refs[:n_pages], refs[n_pages:2 * n_pages]
    o_ref, s_ref = refs[2 * n_pages], refs[2 * n_pages + 1]
    lam = _lambda_value(lp_ref[...], lam_init)
    q = q_ref[...] * (QK_DIM ** -0.5 * LOG2E)
    lo = lax.broadcasted_iota(jnp.int32, q.shape, 1) < QK_DIM
    slope = jnp.exp2(-(lax.broadcasted_iota(jnp.int32, q.shape, 0) + 1).astype(F32)) * LOG2E
    tok_bias = slope[None] * lax.broadcasted_iota(jnp.int32, (page, 1, 1), 0).astype(F32)
    past = n_pages * page

    ri = lax.broadcasted_iota(jnp.int32, (128, 128), 0) < QK_DIM
    ci = lax.broadcasted_iota(jnp.int32, (128, 128), 1) < QK_DIM
    half_sum = jnp.where(ri == ci, 1.0, 0.0).astype(BF16)

    def packed_scores(k):
        prod = (k * q).reshape(-1, 128).astype(BF16)
        return jnp.dot(prod, half_sum, preferred_element_type=F32).reshape(k.shape)

    s_new = packed_scores(kn_ref[...])
    m = s_new
    page_bias = [slope * float(past - p * page) for p in range(n_pages)]
    for p in range(n_pages):
        s = packed_scores(k_refs[p][...]) + tok_bias
        s_ref[p * page:(p + 1) * page] = s
        m = jnp.maximum(m, jnp.max(s, axis=0) - page_bias[p])
    e_new = jnp.exp2(s_new - m)
    l = e_new
    for p in range(n_pages):
        e = jnp.exp2(s_ref[p * page:(p + 1) * page] - (m + page_bias[p])[None])
        s_ref[p * page:(p + 1) * page] = e
        l = l + jnp.sum(e, axis=0)
    r = 1.0 / l
    coef = jnp.where(lo, r, -lam * r)
    w_new = e_new * coef
    acc = (w_new + pltpu.roll(w_new, QK_DIM, 1)) * vn_ref[...]
    for p in range(n_pages):
        w = s_ref[p * page:(p + 1) * page] * coef[None]
        w = w + pltpu.roll(w, QK_DIM, 2)
        acc = acc + jnp.sum(w * v_refs[p][...], axis=0)
    o = acc * lax.rsqrt(jnp.mean(acc * acc, axis=-1, keepdims=True) + RMS_EPS)
    o_ref[...] = o * g_ref[...] * (1.0 - lam_init)


def attn_decode(q, k_new, v_new, cache_k, cache_v, page_table, lam_params, subln_g, layer, lam_init):
    db, n_pages = page_table.shape
    page = cache_k.shape[2]

    def kv_spec(j):
        return pl.BlockSpec((None, None, page, N_HEADS_A, 128), lambda b, pt: (layer, pt[b, j], 0, 0, 0))

    tok = pl.BlockSpec((None, N_HEADS_A, 128), lambda b, pt: (b, 0, 0))
    in_specs = [tok, tok, tok,
                pl.BlockSpec((None, 4, QK_DIM), lambda b, pt: (layer, 0, 0)),
                pl.BlockSpec((None, 1, 128), lambda b, pt: (layer, 0, 0))]
    in_specs += [kv_spec(j) for j in range(n_pages)] * 2
    return pl.pallas_call(
        functools.partial(_attn_decode_kernel, n_pages, page, lam_init),
        grid_spec=pltpu.PrefetchScalarGridSpec(
            num_scalar_prefetch=1, grid=(db,), in_specs=in_specs, out_specs=tok,
            scratch_shapes=[pltpu.VMEM((n_pages * page, N_HEADS_A, 128), F32)]),
        out_shape=jax.ShapeDtypeStruct((db, N_HEADS_A, 128), F32),
        compiler_params=_cparams("parallel"), name="attn_decode",
    )(page_table, q, k_new, v_new, lam_params, subln_g.reshape(-1, 1, 128), *([cache_k] * n_pages),
      *([cache_v] * n_pages))


def _ssd_decode_kernel(layer, first, *refs):
    (xbc_ref, z_ref, dt_ref, cst_ref, ssm_ref, cw_ref, cb_ref, dtb_ref, alog_ref, dsk_ref, ng_ref) = refs[:11]
    y_ref, ssm_out_ref, conv_out_ref = refs[-3:]
    if first:
        for other in range(ssm_out_ref.shape[0]):
            if other != layer:
                ssm_out_ref[other] = jnp.zeros(ssm_out_ref.shape[1:], F32)
        ssm_out_ref = ssm_out_ref.at[layer]
    for s in range(xbc_ref.shape[0]):
        _ssd_decode_one(xbc_ref.at[s], z_ref.at[s], dt_ref.at[s], cst_ref.at[s], ssm_ref.at[s], cw_ref, cb_ref, dtb_ref,
                        alog_ref, dsk_ref, ng_ref, y_ref.at[s], ssm_out_ref.at[s], conv_out_ref.at[s])


def _ssd_decode_one(xbc_ref, z_ref, dt_ref, cst_ref, ssm_ref, cw_ref, cb_ref, dtb_ref, alog_ref, dsk_ref,
                    ng_ref, y_ref, ssm_out_ref, conv_out_ref):
    d_ssd = N_HEADS_S * 64
    hpg = N_HEADS_S // SSD_GROUPS
    xnew = xbc_ref[...]
    cst = cst_ref[...]
    cw = cw_ref[...]
    xc = cb_ref[...] + cw[3:4, :] * xnew
    for j in range(CONV_K - 1):
        xc = xc + cw[j:j + 1, :] * cst[j:j + 1, :]
    xc = _silu(xc)
    conv_out_ref[0:2, :] = cst[1:3, :]
    conv_out_ref[2:3, :] = xnew

    dt = _softplus(dt_ref[...] + dtb_ref[...])
    dec = jnp.exp(dt * (-jnp.exp(alog_ref[...])))
    eye = lax.broadcasted_iota(jnp.int32, (128, 128), 0) == lax.broadcasted_iota(jnp.int32, (128, 128), 1)
    lo = lax.broadcasted_iota(jnp.int32, (1, 128), 1) < 64
    dims_nt = (((1,), (1,)), ((), ()))
    ys = []
    for pr in range(N_HEADS_S // 2):
        h0 = 2 * pr
        g = h0 // hpg
        x_pair = xc[:, h0 * 64:h0 * 64 + 128]
        b_g = xc[:, d_ssd + g * SSD_STATE:d_ssd + (g + 1) * SSD_STATE]
        c_g = xc[:, d_ssd + (SSD_GROUPS + g) * SSD_STATE:d_ssd + (SSD_GROUPS + g + 1) * SSD_STATE]
        dt_pair = jnp.where(lo, dt[:, h0:h0 + 1], dt[:, h0 + 1:h0 + 2])
        xdt_row = x_pair * dt_pair
        xdt_diag = jnp.where(eye, jnp.broadcast_to(xdt_row, (128, 128)), 0.0).astype(BF16)
        upd = jnp.dot(xdt_diag, jnp.broadcast_to(b_g, (128, SSD_STATE)).astype(BF16), preferred_element_type=F32)
        h_prev = ssm_ref[h0:h0 + 2]
        h_new = jnp.concatenate([jnp.broadcast_to(dec[:, h0 + k:h0 + k + 1], (64, SSD_STATE)) * h_prev[k]
                                 for k in range(2)], axis=0) + upd
        ssm_out_ref[h0:h0 + 2] = h_new.reshape(2, 64, SSD_STATE)
        y_rows = lax.dot_general(jnp.broadcast_to(c_g, (8, SSD_STATE)).astype(BF16), h_new.astype(BF16), dims_nt,
                                 preferred_element_type=F32)
        dsk_pair = jnp.where(lo, dsk_ref[:, h0:h0 + 1], dsk_ref[:, h0 + 1:h0 + 2])
        ys.append(y_rows[0:1, :] + dsk_pair * x_pair)
    y = jnp.concatenate(ys, axis=1)
    gz = y * _silu(z_ref[...])
    half = d_ssd // SSD_GROUPS
    outs = []
    for g in range(SSD_GROUPS):
        part = gz[:, g * half:(g + 1) * half]
        outs.append(part * lax.rsqrt(jnp.mean(part * part, axis=1, keepdims=True) + RMS_EPS))
    y_ref[...] = jnp.concatenate(outs, axis=1) * ng_ref[...]


def ssd_decode(xbc, z, dt_raw, state_conv, state_ssm, conv_w, conv_b, dt_bias, a_log, d_skip, norm_g, layer,
               ssm_stack=None):
    db, cdim = xbc.shape
    depth = state_ssm.shape[0]
    d_ssd = N_HEADS_S * 64
    bb = SSD_DECODE_SEQS
    assert db % bb == 0
    one = lambda width: pl.BlockSpec((bb, 1, width), lambda b: (b, 0, 0))
    vec = lambda width: pl.BlockSpec((1, width), lambda b: (0, 0))
    ssm_shape = state_ssm.shape[2:]
    first = ssm_stack is None
    in_specs = [one(cdim), one(d_ssd), one(128),
                pl.BlockSpec((None, bb, CONV_K - 1, cdim), lambda b: (layer, b, 0, 0)),
                pl.BlockSpec((None, bb) + ssm_shape, lambda b: (layer, b, 0, 0, 0)),
                pl.BlockSpec((None, CONV_K, cdim), lambda b: (layer, 0, 0)),
                vec(cdim), vec(128), vec(128), vec(128), vec(d_ssd)]
    args = [xbc.reshape(db, 1, cdim), z.reshape(db, 1, d_ssd), dt_raw.reshape(db, 1, 128), state_conv, state_ssm,
            conv_w, conv_b[layer].reshape(1, -1), _pad_lanes(dt_bias[layer]), _pad_lanes(a_log[layer]),
            _pad_lanes(d_skip[layer]), norm_g[layer].reshape(1, -1)]
    if first:
        ssm_out = pl.BlockSpec((depth, bb) + ssm_shape, lambda b: (0, b, 0, 0, 0))
        aliases = {}
    else:
        ssm_out = pl.BlockSpec((None, bb) + ssm_shape, lambda b: (layer, b, 0, 0, 0))
        in_specs.append(pl.BlockSpec(memory_space=pl.ANY))
        args.append(ssm_stack)
        aliases = {len(args) - 1: 1}
    return pl.pallas_call(
        functools.partial(_ssd_decode_kernel, layer, first), grid=(db // bb,),
        in_specs=in_specs,
        out_specs=[one(d_ssd), ssm_out, pl.BlockSpec((bb, CONV_K - 1, cdim), lambda b: (b, 0, 0))],
        out_shape=[jax.ShapeDtypeStruct((db, 1, d_ssd), F32),
                   jax.ShapeDtypeStruct((depth, db) + ssm_shape, F32),
                   jax.ShapeDtypeStruct((db, CONV_K - 1, cdim), F32)],
        input_output_aliases=aliases,
        compiler_params=_cparams("arbitrary"), name="ssd_decode",
    )(*args)


def _mem_attn_prompt_kernel(q_ref, k_ref, v_ref, init_ref, o_ref, kbf_ref, vbf_ref):
    del init_ref

    @pl.when(pl.program_id(1) == 0)
    def _():
        kbf_ref[...] = k_ref[...].astype(BF16)
        vbf_ref[...] = v_ref[...].astype(BF16)

    dh = q_ref.shape[1] // N_HEADS_MEM
    dims_nt = (((1,), (1,)), ((), ()))
    for h in range(N_HEADS_MEM):
        sl = slice(h * dh, (h + 1) * dh)
        s = lax.dot_general(q_ref[:, sl], kbf_ref[:, sl], dims_nt, preferred_element_type=F32) * (dh ** -0.5)
        e = jnp.exp(s - jnp.max(s, axis=1, keepdims=True))
        p = e / jnp.sum(e, axis=1, keepdims=True)
        o_ref[:, sl] = jnp.dot(p.astype(BF16), vbf_ref[:, sl], preferred_element_type=F32).astype(o_ref.dtype)


def mem_attn_prompt(qm, kv, batch, seq, n_mem, out_init, tq=512):
    d = qm.shape[1]
    nq = seq // tq
    return pl.pallas_call(
        _mem_attn_prompt_kernel, grid=(batch, nq),
        in_specs=[pl.BlockSpec((tq, d), lambda b, i: (b * nq + i, 0)),
                  pl.BlockSpec((n_mem, d), lambda b, i: (b, 0)),
                  pl.BlockSpec((n_mem, d), lambda b, i: (b, 1)),
                  pl.BlockSpec(memory_space=pl.ANY)],
        out_specs=pl.BlockSpec((tq, d), lambda b, i: (b * nq + i, 0)),
        out_shape=jax.ShapeDtypeStruct(out_init.shape, out_init.dtype),
        input_output_aliases={3: 0},
        scratch_shapes=[pltpu.VMEM((n_mem, d), BF16), pltpu.VMEM((n_mem, d), BF16)],
        compiler_params=_cparams("parallel", "arbitrary"), name="mem_attn_prompt",
    )(qm, kv, kv, out_init)


def _mem_attn_decode_kernel(q_ref, k_ref, v_ref, o_ref):
    for s in range(q_ref.shape[0]):
        q = q_ref[s]
        sc = jnp.sum(k_ref[s] * q[None], axis=-1, keepdims=True) * (q.shape[-1] ** -0.5)
        e = jnp.exp(sc - jnp.max(sc, axis=0, keepdims=True))
        p = e / jnp.sum(e, axis=0, keepdims=True)
        o_ref[s] = jnp.sum(p * v_ref[s], axis=0)


def mem_attn_decode(q, cache_mem_k, cache_mem_v, layer, bb=2):
    db = q.shape[0]
    assert db % bb == 0
    blk = cache_mem_k.shape[2:]
    tok = pl.BlockSpec((bb,) + q.shape[1:], lambda b: (b, 0, 0))
    kv = pl.BlockSpec((None, bb) + blk, lambda b: (layer, b, 0, 0, 0))
    return pl.pallas_call(
        _mem_attn_decode_kernel, grid=(db // bb,), in_specs=[tok, kv, kv], out_specs=tok,
        out_shape=jax.ShapeDtypeStruct(q.shape, F32),
        compiler_params=_cparams("parallel"), name="mem_attn_decode",
    )(q, cache_mem_k, cache_mem_v)


def _router_kernel(x_ref, w_ref, idx_ref, gate_ref):
    logits = jnp.dot(x_ref[...], w_ref[...], preferred_element_type=F32, precision=lax.Precision.HIGHEST)
    n_exp = logits.shape[1]
    lane = lax.broadcasted_iota(jnp.int32, logits.shape, 1)
    m1 = jnp.max(logits, axis=1, keepdims=True)
    i1 = jnp.min(jnp.where(logits == m1, lane, n_exp), axis=1, keepdims=True)
    rest = jnp.where(lane == i1, -jnp.inf, logits)
    m2 = jnp.max(rest, axis=1, keepdims=True)
    i2 = jnp.min(jnp.where(rest == m2, lane, n_exp), axis=1, keepdims=True)
    e2 = jnp.exp(m2 - m1)
    idx_ref[:, 0:1] = i1
    idx_ref[:, 1:2] = i2
    gate_ref[:, 0:1] = 1.0 / (1.0 + e2)
    gate_ref[:, 1:2] = e2 / (1.0 + e2)


def router_top2(x, w_router, bm):
    m, d = x.shape
    n_exp = w_router.shape[-1]
    row = lambda w: pl.BlockSpec((bm, w), lambda i: (i, 0))
    return pl.pallas_call(
        _router_kernel, grid=(m // bm,),
        in_specs=[row(d), pl.BlockSpec((d, n_exp), lambda i: (0, 0))],
        out_specs=[row(TOP_K), row(TOP_K)],
        out_shape=[jax.ShapeDtypeStruct((m, TOP_K), jnp.int32), jax.ShapeDtypeStruct((m, TOP_K), F32)],
        compiler_params=_cparams("parallel"), name="router",
    )(x, w_router)


def _moe_plan(idx, n_exp, bm, n_tiles):
    e_flat = idx.reshape(-1)
    onehot = (e_flat[:, None] == jnp.arange(n_exp, dtype=jnp.int32)[None, :]).astype(jnp.int32)
    csum = jnp.cumsum(onehot, axis=0)
    counts = csum[-1]
    padded = ((counts + bm - 1) // bm) * bm
    gend = jnp.cumsum(padded)
    gstart = gend - padded
    dest = jnp.sum(onehot * (gstart[None, :] + csum - 1), axis=1).astype(jnp.int32)
    tile_start = jnp.arange(n_tiles, dtype=jnp.int32) * bm
    tile_expert = jnp.sum((tile_start[:, None] >= gend[None, :]).astype(jnp.int32), axis=1)
    tile_expert = jnp.minimum(tile_expert, n_exp - 1).astype(jnp.int32)
    n_valid = (gend[-1] // bm).astype(jnp.int32).reshape(1)
    ids = jnp.arange(n_exp, dtype=jnp.int32)
    later = jnp.logical_and(ids[None, :] > ids[:, None], (counts > 0)[None, :])
    first = jnp.min(jnp.where(counts > 0, ids, n_exp))
    nxt = jnp.min(jnp.where(later, ids[None, :], n_exp), axis=1)
    wrap = (nxt == n_exp).astype(jnp.int32)
    nxt = jnp.where(nxt == n_exp, first, nxt).astype(jnp.int32)
    return dest, (tile_expert, nxt[tile_expert], wrap[tile_expert], n_valid)


def _row_copy(src_ref, src_row, dst_ref, dst_row, sem):
    return pltpu.make_async_copy(src_ref.at[pl.ds(src_row, 1), :], dst_ref.at[pl.ds(dst_row, 1), :], sem)


def _moe_scatter_kernel(bm, dest_ref, x_ref, xs_in_ref, xs_ref, sem):
    del xs_in_ref
    base = pl.program_id(0) * bm * TOP_K

    def start(r, _):
        for k in range(TOP_K):
            _row_copy(x_ref, r, xs_ref, dest_ref[base + r * TOP_K + k], sem).start()
        return 0

    def wait(r, _):
        for k in range(TOP_K):
            _row_copy(x_ref, 0, xs_ref, 0, sem).wait()
        return 0

    lax.fori_loop(0, bm, start, 0)
    lax.fori_loop(0, bm, wait, 0)


def moe_scatter(x, dest, n_rows, bm):
    m, d = x.shape
    zeros = jnp.zeros((n_rows, d), x.dtype)
    return pl.pallas_call(
        functools.partial(_moe_scatter_kernel, bm),
        grid_spec=pltpu.PrefetchScalarGridSpec(
            num_scalar_prefetch=1, grid=(m // bm,),
            in_specs=[pl.BlockSpec((bm, d), lambda i, dest: (i, 0)), pl.BlockSpec(memory_space=pl.ANY)],
            out_specs=pl.BlockSpec(memory_space=pl.ANY),
            scratch_shapes=[pltpu.SemaphoreType.DMA(())]),
        out_shape=jax.ShapeDtypeStruct((n_rows, d), x.dtype),
        input_output_aliases={2: 0},
        compiler_params=_cparams("arbitrary"), name="moe_scatter",
    )(dest, x, zeros)


def _expert_changed(te_ref, i):
    return jnp.logical_or(i == 0, te_ref[i] != te_ref[jnp.maximum(i - 1, 0)])


def _weight_block_copies(w_hbm_refs, stage_ref, sem, e, j, bn):
    col = pl.multiple_of(j * bn, bn)
    return [pltpu.make_async_copy(w.at[0, e, :, pl.ds(col, bn)], stage_ref.at[k], sem.at[k])
            for k, w in enumerate(w_hbm_refs)]


def _gmm_weights_step(te_ref, nx_ref, wrap_ref, nv_ref, w_hbm_refs, stage_ref, wbf_ref, sem, bn):
    j, i = pl.program_id(0), pl.program_id(1)

    @pl.when(jnp.logical_and(j == 0, i == 0))
    def _():
        for c in _weight_block_copies(w_hbm_refs, stage_ref, sem, te_ref[0], 0, bn):
            c.start()

    @pl.when(jnp.logical_and(i < nv_ref[0], _expert_changed(te_ref, i)))
    def _():
        for c in _weight_block_copies(w_hbm_refs, stage_ref, sem, te_ref[i], j, bn):
            c.wait()
        wbf_ref[...] = stage_ref[...].astype(BF16)
        nj = j + wrap_ref[i]

        @pl.when(nj < pl.num_programs(0))
        def _():
            for c in _weight_block_copies(w_hbm_refs, stage_ref, sem, nx_ref[i], nj, bn):
                c.start()


def _gmm_up_kernel(bn, te_ref, nx_ref, wrap_ref, nv_ref, x_ref, wg_ref, wu_ref, o_ref, stage_ref, wbf_ref, sem):
    i = pl.program_id(1)
    _gmm_weights_step(te_ref, nx_ref, wrap_ref, nv_ref, (wg_ref, wu_ref), stage_ref, wbf_ref, sem, bn)

    @pl.when(i < nv_ref[0])
    def _():
        x = x_ref[...].astype(BF16)
        g = jnp.dot(x, wbf_ref[0], preferred_element_type=F32)
        u = jnp.dot(x, wbf_ref[1], preferred_element_type=F32)
        o_ref[...] = (g * jax.nn.sigmoid(g) * u).astype(o_ref.dtype)

    @pl.when(i >= nv_ref[0])
    def _():
        o_ref[...] = jnp.zeros_like(o_ref)


def gmm_up(xs, wg, wu, plan, bm, bn):
    rows, k = xs.shape
    f = wg.shape[-1]
    hbm = pl.BlockSpec(memory_space=pl.ANY)
    return pl.pallas_call(
        functools.partial(_gmm_up_kernel, bn),
        grid_spec=pltpu.PrefetchScalarGridSpec(
            num_scalar_prefetch=4, grid=(f // bn, rows // bm),
            in_specs=[pl.BlockSpec((bm, k), lambda j, i, *_: (i, 0)), hbm, hbm],
            out_specs=pl.BlockSpec((bm, bn), lambda j, i, *_: (i, j)),
            scratch_shapes=[pltpu.VMEM((2, k, bn), F32), pltpu.VMEM((2, k, bn), BF16),
                            pltpu.SemaphoreType.DMA((2,))]),
        out_shape=jax.ShapeDtypeStruct((rows, f), BF16),
        compiler_params=_cparams("arbitrary", "arbitrary"), name="gmm_up",
    )(*plan, xs, wg, wu)


def _gmm_down_kernel(bn, te_ref, nx_ref, wrap_ref, nv_ref, a_ref, w_ref, o_ref, stage_ref, wbf_ref, sem):
    i = pl.program_id(1)
    _gmm_weights_step(te_ref, nx_ref, wrap_ref, nv_ref, (w_ref,), stage_ref, wbf_ref, sem, bn)

    @pl.when(i < nv_ref[0])
    def _():
        o_ref[...] = jnp.dot(a_ref[...], wbf_ref[0], preferred_element_type=F32)

    @pl.when(i >= nv_ref[0])
    def _():
        o_ref[...] = jnp.zeros_like(o_ref)


def gmm_down(a, wd, plan, bm, bn):
    rows, f = a.shape
    d = wd.shape[-1]
    return pl.pallas_call(
        functools.partial(_gmm_down_kernel, bn),
        grid_spec=pltpu.PrefetchScalarGridSpec(
            num_scalar_prefetch=4, grid=(d // bn, rows // bm),
            in_specs=[pl.BlockSpec((bm, f), lambda j, i, *_: (i, 0)), pl.BlockSpec(memory_space=pl.ANY)],
            out_specs=pl.BlockSpec((bm, bn), lambda j, i, *_: (i, j)),
            scratch_shapes=[pltpu.VMEM((1, f, bn), F32), pltpu.VMEM((1, f, bn), BF16),
                            pltpu.SemaphoreType.DMA((1,))]),
        out_shape=jax.ShapeDtypeStruct((rows, d), F32),
        compiler_params=_cparams("arbitrary", "arbitrary"), name="gmm_down",
    )(*plan, a, wd)


def _moe_combine_kernel(bm, alpha, dest_ref, x_ref, gate_ref, g_ref, b_ref, ys_ref, o_ref, obf_ref, buf_ref, sem):
    base = pl.program_id(0) * bm * TOP_K

    def start(r, _):
        for k in range(TOP_K):
            _row_copy(ys_ref, dest_ref[base + r * TOP_K + k], buf_ref.at[k], r, sem).start()
        return 0

    def wait(r, _):
        for k in range(TOP_K):
            _row_copy(ys_ref, 0, buf_ref.at[k], 0, sem).wait()
        return 0

    lax.fori_loop(0, bm, start, 0)
    lax.fori_loop(0, bm, wait, 0)
    f = gate_ref[:, 0:1] * buf_ref[0]
    for k in range(1, TOP_K):
        f = f + gate_ref[:, k:k + 1] * buf_ref[k]
    y = _ln_rows(alpha * x_ref[...] + f, g_ref[...], b_ref[...])
    o_ref[...] = y
    obf_ref[...] = y.astype(BF16)


def moe_combine(x, gates, ys, dest, g, b, alpha, bm):
    m, d = x.shape
    row = lambda w: pl.BlockSpec((bm, w), lambda i, dest: (i, 0))
    vec = pl.BlockSpec((1, d), lambda i, dest: (0, 0))
    return pl.pallas_call(
        functools.partial(_moe_combine_kernel, bm, alpha),
        grid_spec=pltpu.PrefetchScalarGridSpec(
            num_scalar_prefetch=1, grid=(m // bm,),
            in_specs=[row(d), row(TOP_K), vec, vec, pl.BlockSpec(memory_space=pl.ANY)],
            out_specs=[row(d), row(d)],
            scratch_shapes=[pltpu.VMEM((TOP_K, bm, d), F32), pltpu.SemaphoreType.DMA(())]),
        out_shape=[jax.ShapeDtypeStruct((m, d), F32), jax.ShapeDtypeStruct((m, d), BF16)],
        compiler_params=_cparams("arbitrary"), name="moe_combine",
    )(dest, x, gates, g.reshape(1, d), b.reshape(1, d), ys)


def moe_ffn_ln(x, w_router, wg, wu, wd, g, b, alpha, moe_layer):
    m, d = x.shape
    n_exp = w_router.shape[-1]
    bm_tok = _row_block(m, 320)
    bm = 512
    n_tiles = -(-(m * TOP_K + n_exp * (bm - 1)) // bm)
    idx, gates = router_top2(x, w_router[moe_layer], bm_tok)
    dest, plan = _moe_plan(idx, n_exp, bm, n_tiles)
    xs = moe_scatter(x, dest, n_tiles * bm, bm_tok)
    a = gmm_up(xs, wg[moe_layer:moe_layer + 1], wu[moe_layer:moe_layer + 1], plan, bm, min(1024, wg.shape[-1]))
    ys = gmm_down(a, wd[moe_layer:moe_layer + 1], plan, bm, 512)
    return moe_combine(x, gates, ys, dest, g, b, alpha, bm_tok)


def kernel(x_prompt, x_sample, mem_prompt, cache_k, cache_v, cache_mem_k, cache_mem_v, state_ssm, state_conv, page_table, ln_in_g, ln_in_b, w_in, conv_w, conv_b, dt_bias, a_log, d_skip, ssd_norm_g, lam_params, subln_g, w_out, w_mem_q, w_mem_kv, w_mem_o, ln_g, ln_b, w_ff_gate, w_ff_up, w_ff_down, w_router, w_exp_gate, w_exp_up, w_exp_down):
    batch, seq, d = x_prompt.shape
    db = x_sample.shape[0]
    assert x_sample.shape[1] == 1
    depth = w_in.shape[0]
    n_mem = mem_prompt.shape[1]
    mp = batch * seq
    m = mp + db
    alpha = (2 * depth) ** 0.25
    d_attn = N_HEADS_A * 128
    d_ssd = N_HEADS_S * 64
    cdim = d_ssd + 2 * SSD_GROUPS * SSD_STATE
    n_main = 3 * d_attn + d_ssd + cdim
    bm = _row_block(m, 1664)
    bm_small = _row_block(m, 416)

    x_all = jnp.concatenate([x_prompt.reshape(mp, d), x_sample.reshape(db, d)], axis=0)
    x, x_bf = layer_norm_in(x_all, ln_in_g, ln_in_b)
    mem2d = mem_prompt.reshape(batch * n_mem, d)

    w_in_nk = jnp.swapaxes(w_in, 1, 2)
    outs = {k: [] for k in ("sp", "cp", "mkp", "mvp", "ks", "vs", "cs")}
    ssm_s = None
    u_layers = []
    for l in range(depth):
        lam_init = 0.8 - 0.6 * math.exp(-0.3 * l)
        u = matmul(x_bf, w_in_nk, lead=(l,), bm=bm, bn=512, w_is_nk=True, name="mm_in")
        us = u[mp:]
        heads = lambda a: a.reshape(db, N_HEADS_A, 128)
        o_a_s = attn_decode(heads(us[:, :d_attn]), heads(us[:, d_attn:2 * d_attn]), heads(us[:, 2 * d_attn:3 * d_attn]),
                            cache_k, cache_v, page_table, lam_params, subln_g, l, lam_init)
        dt_s = jnp.pad(us[:, n_main:], ((0, 0), (0, 128 - N_HEADS_S)))
        y_s_s, ssm_s, conv_s = ssd_decode(us[:, n_main - cdim:n_main], us[:, 3 * d_attn:3 * d_attn + d_ssd], dt_s,
                                          state_conv, state_ssm, conv_w, conv_b, dt_bias, a_log, d_skip, ssd_norm_g, l, ssm_s)
        mix_s = jnp.concatenate([o_a_s.reshape(db, d_attn), y_s_s.reshape(db, d_ssd)], axis=1).astype(BF16)
        mix = rows_below_zeros(mix_s, m)
        mix = attn_prompt(u, lam_params, subln_g, l, batch, seq, lam_init, mix)
        mix, ssm_p, conv_p = ssd_prompt(u, conv_w, conv_b, dt_bias, a_log, d_skip, ssd_norm_g, l, batch, seq, mix)
        x, x_bf = matmul_add_ln(mix, w_out, x, ln_g[l, 0], ln_b[l, 0], alpha, lead=(l,), bm=bm_small, name="mm_out_ln")
        qm = matmul(x_bf, w_mem_q, lead=(l,), bm=bm, bn=512, out_dtype=BF16, name="mm_mem_q")
        kv = matmul(mem2d, w_mem_kv, lead=(l,), bm=_row_block(batch * n_mem, 512), bn=512, name="mm_mem_kv")
        c_s = mem_attn_decode(qm[mp:].astype(F32).reshape(db, N_HEADS_MEM, d // N_HEADS_MEM), cache_mem_k, cache_mem_v, l)
        c = rows_below_zeros(c_s.reshape(db, d).astype(BF16), m)
        c = mem_attn_prompt(qm, kv, batch, seq, n_mem, c)
        x, x_bf = matmul_add_ln(c, w_mem_o, x, ln_g[l, 1], ln_b[l, 1], alpha, lead=(l,), bm=bm_small, name="mm_mem_o_ln")
        if l % 2 == 0:
            act = swiglu_up(x_bf, w_ff_gate, w_ff_up, lead=(l // 2,), bm=bm, bn=512)
            a = matmul(act, w_ff_down, lead=(l // 2,), bm=bm_small, bn=512, name="mm_ff_down")
            x, x_bf = add_layer_norm(x, a, ln_g[l, 2], ln_b[l, 2], alpha)
        else:
            x, x_bf = moe_ffn_ln(x, w_router, w_exp_gate, w_exp_up, w_exp_down, ln_g[l, 2], ln_b[l, 2], alpha, l // 2)
        u_layers.append(u)
        outs["sp"].append(ssm_p.reshape(batch, N_HEADS_S, 64, SSD_STATE))
        outs["cp"].append(conv_p[:, 8 - (CONV_K - 1):, :])
        outs["mkp"].append(kv[:, :d].reshape(batch, n_mem, N_HEADS_MEM, d // N_HEADS_MEM))
        outs["mvp"].append(kv[:, d:].reshape(batch, n_mem, N_HEADS_MEM, d // N_HEADS_MEM))
        outs["ks"].append(us[:, d_attn:2 * d_attn].reshape(db, 1, N_HEADS_A, 128))
        outs["vs"].append(us[:, 2 * d_attn:3 * d_attn].reshape(db, 1, N_HEADS_A, 128))
        outs["cs"].append(conv_s)
    st = {k: jnp.stack(v) for k, v in outs.items()}
    kp, vp = kv_prompt_outputs(u_layers, mp)
    st["ss"] = ssm_s
    st["kp"] = kp.reshape(depth, batch, seq, N_HEADS_A, 128)
    st["vp"] = vp.reshape(depth, batch, seq, N_HEADS_A, 128)
    return (x[:mp].reshape(batch, seq, d), x[mp:].reshape(db, 1, d),
            st["kp"], st["vp"], st["sp"], st["cp"], st["mkp"], st["mvp"], st["ks"], st["vs"], st["ss"], st["cs"])
```

```python
import functools
import math

import jax
import jax.numpy as jnp
from jax import lax
from jax.experimental import pallas as pl
from jax.experimental.pallas import tpu as pltpu

F32 = jnp.float32
BF16 = jnp.bfloat16

LN_EPS = 1e-5
RMS_EPS = 1e-5
N_HEADS_A = 8
QK_DIM = 64
N_HEADS_S = 16
SSD_GROUPS = 2
SSD_STATE = 128
SSD_CHUNK = 128
CONV_K = 4
N_HEADS_MEM = 4
TOP_K = 2
SSD_DECODE_SEQS = 4
VMEM_LIMIT = 56 * 1024 * 1024


def _cparams(*sem):
    return pltpu.CompilerParams(dimension_semantics=sem, vmem_limit_bytes=VMEM_LIMIT)


def _row_block(m, target):
    best = None
    for d in range(16, min(m, target) + 1, 16):
        if m % d == 0:
            best = d
    assert best is not None, (m, target)
    return best


def _ln_rows(x, g, b):
    mu = jnp.mean(x, axis=-1, keepdims=True)
    xc = x - mu
    var = jnp.mean(xc * xc, axis=-1, keepdims=True)
    return xc * lax.rsqrt(var + LN_EPS) * g + b


def _ln_kernel(x_ref, g_ref, b_ref, o_ref, obf_ref):
    y = _ln_rows(x_ref[...], g_ref[...], b_ref[...])
    o_ref[...] = y
    obf_ref[...] = y.astype(BF16)


def layer_norm_in(x, g, b):
    m, d = x.shape
    bm = _row_block(m, 512)
    row = pl.BlockSpec((bm, d), lambda i: (i, 0))
    vec = pl.BlockSpec((1, d), lambda i: (0, 0))
    return pl.pallas_call(
        _ln_kernel, grid=(m // bm,), in_specs=[row, vec, vec], out_specs=[row, row],
        out_shape=[jax.ShapeDtypeStruct((m, d), F32), jax.ShapeDtypeStruct((m, d), BF16)],
        compiler_params=_cparams("parallel"), name="ln_in",
    )(x, g.reshape(1, d), b.reshape(1, d))


def _add_ln_kernel(alpha, x_ref, a_ref, g_ref, b_ref, o_ref, obf_ref):
    y = _ln_rows(alpha * x_ref[...] + a_ref[...], g_ref[...], b_ref[...])
    o_ref[...] = y
    obf_ref[...] = y.astype(BF16)


def add_layer_norm(x, a, g, b, alpha):
    m, d = x.shape
    bm = _row_block(m, 512)
    row = pl.BlockSpec((bm, d), lambda i: (i, 0))
    vec = pl.BlockSpec((1, d), lambda i: (0, 0))
    return pl.pallas_call(
        functools.partial(_add_ln_kernel, alpha), grid=(m // bm,),
        in_specs=[row, row, vec, vec], out_specs=[row, row],
        out_shape=[jax.ShapeDtypeStruct((m, d), F32), jax.ShapeDtypeStruct((m, d), BF16)],
        compiler_params=_cparams("parallel"), name="add_ln",
    )(x, a, g.reshape(1, d), b.reshape(1, d))


def _rows_below_zeros_kernel(x_ref, o_ref):
    last = pl.num_programs(0) - 1
    n = x_ref.shape[0]

    @pl.when(pl.program_id(0) != last)
    def _():
        o_ref[...] = jnp.zeros_like(o_ref)

    @pl.when(pl.program_id(0) == last)
    def _():
        o_ref[0:o_ref.shape[0] - n, :] = jnp.zeros((o_ref.shape[0] - n, o_ref.shape[1]), o_ref.dtype)
        o_ref[o_ref.shape[0] - n:, :] = x_ref[...]


def rows_below_zeros(x, m):
    n, d = x.shape
    half = m // 2
    assert m % 2 == 0 and half % 16 == 0 and n % 16 == 0 and n <= half
    return pl.pallas_call(
        _rows_below_zeros_kernel, grid=(2,),
        in_specs=[pl.BlockSpec((n, d), lambda i: (0, 0))],
        out_specs=pl.BlockSpec((half, d), lambda i: (i, 0)),
        out_shape=jax.ShapeDtypeStruct((m, d), x.dtype),
        compiler_params=_cparams("arbitrary"), name="rows_below_zeros",
    )(x)


def _mm_kernel(w_is_nk, x_ref, w_ref, o_ref, wbf_ref):
    @pl.when(pl.program_id(1) == 0)
    def _():
        wbf_ref[...] = w_ref[...].astype(BF16)

    dims = (((1,), (1 if w_is_nk else 0,)), ((), ()))
    o_ref[...] = lax.dot_general(x_ref[...].astype(BF16), wbf_ref[...], dims,
                                 preferred_element_type=F32).astype(o_ref.dtype)


def matmul(x, w, *, lead=(), col0=0, ncols=None, bm, bn, out_dtype=F32, w_is_nk=False, name="mm"):
    m, k = x.shape
    n_axis, k_axis = (-2, -1) if w_is_nk else (-1, -2)
    n = w.shape[n_axis] - col0 if ncols is None else ncols
    assert w.shape[k_axis] == k and m % bm == 0 and col0 % bn == 0
    nl = len(lead)
    cb0 = col0 // bn
    if w_is_nk:
        w_spec = pl.BlockSpec((None,) * nl + (bn, k), lambda j, i: tuple(lead) + (cb0 + j, 0))
    else:
        w_spec = pl.BlockSpec((None,) * nl + (k, bn), lambda j, i: tuple(lead) + (0, cb0 + j))
    return pl.pallas_call(
        functools.partial(_mm_kernel, w_is_nk), grid=(pl.cdiv(n, bn), m // bm),
        in_specs=[pl.BlockSpec((bm, k), lambda j, i: (i, 0)), w_spec],
        out_specs=pl.BlockSpec((bm, bn), lambda j, i: (i, j)),
        out_shape=jax.ShapeDtypeStruct((m, n), out_dtype),
        scratch_shapes=[pltpu.VMEM((bn, k) if w_is_nk else (k, bn), BF16)],
        compiler_params=_cparams("parallel", "arbitrary"), name=name,
    )(x, w)


def _mm_add_ln_kernel(alpha, x_ref, w_ref, r_ref, g_ref, b_ref, o_ref, obf_ref, wbf_ref):
    @pl.when(pl.program_id(0) == 0)
    def _():
        wbf_ref[...] = w_ref[...].astype(BF16)

    a = jnp.dot(x_ref[...], wbf_ref[...], preferred_element_type=F32)
    y = _ln_rows(alpha * r_ref[...] + a, g_ref[...], b_ref[...])
    o_ref[...] = y
    obf_ref[...] = y.astype(BF16)


def matmul_add_ln(x, w, resid, g, b, alpha, *, lead, bm, name):
    m, k = x.shape
    n = w.shape[-1]
    assert m % bm == 0 and resid.shape == (m, n)
    row = lambda width: pl.BlockSpec((bm, width), lambda i: (i, 0))
    vec = pl.BlockSpec((1, n), lambda i: (0, 0))
    w_spec = pl.BlockSpec((None,) * len(lead) + (k, n), lambda i: tuple(lead) + (0, 0), pipeline_mode=pl.Buffered(1))
    return pl.pallas_call(
        functools.partial(_mm_add_ln_kernel, alpha), grid=(m // bm,),
        in_specs=[row(k), w_spec, row(n), vec, vec], out_specs=[row(n), row(n)],
        out_shape=[jax.ShapeDtypeStruct((m, n), F32), jax.ShapeDtypeStruct((m, n), BF16)],
        scratch_shapes=[pltpu.VMEM((k, n), BF16)],
        compiler_params=_cparams("arbitrary"), name=name,
    )(x, w, resid, g.reshape(1, n), b.reshape(1, n))


def _swiglu_up_kernel(x_ref, wg_ref, wu_ref, o_ref, wgbf_ref, wubf_ref):
    @pl.when(pl.program_id(1) == 0)
    def _():
        wgbf_ref[...] = wg_ref[...].astype(BF16)
        wubf_ref[...] = wu_ref[...].astype(BF16)

    x = x_ref[...]
    g = jnp.dot(x, wgbf_ref[...], preferred_element_type=F32)
    u = jnp.dot(x, wubf_ref[...], preferred_element_type=F32)
    o_ref[...] = (g * jax.nn.sigmoid(g) * u).astype(o_ref.dtype)


def swiglu_up(x, wg, wu, *, lead, bm, bn):
    m, k = x.shape
    n = wg.shape[-1]
    assert m % bm == 0 and n % bn == 0
    nl = len(lead)
    w_spec = pl.BlockSpec((None,) * nl + (k, bn), lambda j, i: tuple(lead) + (0, j))
    return pl.pallas_call(
        _swiglu_up_kernel, grid=(n // bn, m // bm),
        in_specs=[pl.BlockSpec((bm, k), lambda j, i: (i, 0)), w_spec, w_spec],
        out_specs=pl.BlockSpec((bm, bn), lambda j, i: (i, j)),
        out_shape=jax.ShapeDtypeStruct((m, n), BF16),
        scratch_shapes=[pltpu.VMEM((k, bn), BF16), pltpu.VMEM((k, bn), BF16)],
        compiler_params=_cparams("parallel", "arbitrary"), name="swiglu_up",
    )(x, wg, wu)


def _lambda_value(lp, lam_init):
    t1 = jnp.sum(lp[0:1, :] * lp[1:2, :], axis=1, keepdims=True)
    t2 = jnp.sum(lp[2:3, :] * lp[3:4, :], axis=1, keepdims=True)
    return jnp.exp(t1) - jnp.exp(t2) + lam_init


LOG2E = 1.4426950408889634


def _attn_prompt_kernel(tq, lam_init, q_ref, k_ref, v_ref, lp_ref, g_ref, mix_ref, o_ref, kbf_ref, vt_ref):
    del mix_ref
    h = pl.program_id(1)
    qi = pl.program_id(2)

    @pl.when(qi == 0)
    def _():
        kbf_ref[...] = k_ref[...].astype(BF16)
        for c in range(vt_ref.shape[0]):
            vt_ref[c] = v_ref[c * tq:(c + 1) * tq, :].T.astype(BF16)

    slope = jnp.exp2(-(h + 1).astype(F32) * jnp.ones((1, 1), F32)) * LOG2E
    lam = _lambda_value(lp_ref[...], lam_init)
    qt = (q_ref[...] * (QK_DIM ** -0.5 * LOG2E)).T
    sub = lax.broadcasted_iota(jnp.int32, qt.shape, 0)
    qts = (jnp.where(sub < QK_DIM, qt, 0.0).astype(BF16), jnp.where(sub >= QK_DIM, qt, 0.0).astype(BF16))
    krow = lax.broadcasted_iota(jnp.int32, (tq, tq), 0)
    qcol = lax.broadcasted_iota(jnp.int32, (tq, tq), 1)
    base = -slope * (qcol - krow).astype(F32)

    def chunk(kj, carry, masked):
        kc = kbf_ref[pl.ds(pl.multiple_of(kj * tq, tq), tq), :]
        vtc = vt_ref[kj]
        off = -slope * ((qi - kj) * tq).astype(F32)
        out = []
        for qm, (m, l, acc) in zip(qts, carry):
            t = jnp.dot(kc, qm, preferred_element_type=F32) + base
            if masked:
                t = jnp.where(krow <= qcol, t, -jnp.inf)
            m_new = jnp.maximum(m, jnp.max(t, axis=0, keepdims=True) + off)
            p = jnp.exp2(t - (m_new - off))
            alpha = jnp.exp2(m - m_new)
            l_new = alpha * l + jnp.sum(p, axis=0, keepdims=True)
            acc_new = alpha * acc + jnp.dot(vtc, p.astype(BF16), preferred_element_type=F32)
            out.append((m_new, l_new, acc_new))
        return tuple(out)

    init_one = (jnp.full((1, tq), -1e30, F32), jnp.zeros((1, tq), F32), jnp.zeros((128, tq), F32))
    carry = lax.fori_loop(0, qi, lambda kj, c: chunk(kj, c, False), (init_one, init_one))
    (_, l1, a1), (_, l2, a2) = chunk(qi, carry, True)
    o = (a1 / l1 - lam * (a2 / l2)).T
    o = o * lax.rsqrt(jnp.mean(o * o, axis=1, keepdims=True) + RMS_EPS)
    o_ref[...] = (o * g_ref[...] * (1.0 - lam_init)).astype(o_ref.dtype)


def attn_prompt(u, lam_params, subln_g, layer, batch, seq, lam_init, mix, tq=512):
    nq = seq // tq
    lp_spec = pl.BlockSpec((None, 4, QK_DIM), lambda b, h, i: (layer, 0, 0))
    g_spec = pl.BlockSpec((None, 1, 128), lambda b, h, i: (layer, 0, 0))
    return pl.pallas_call(
        functools.partial(_attn_prompt_kernel, tq, lam_init),
        grid=(batch, N_HEADS_A, nq),
        in_specs=[pl.BlockSpec((tq, 128), lambda b, h, i: (b * nq + i, h)),
                  pl.BlockSpec((seq, 128), lambda b, h, i: (b, N_HEADS_A + h)),
                  pl.BlockSpec((seq, 128), lambda b, h, i: (b, 2 * N_HEADS_A + h)),
                  lp_spec, g_spec, pl.BlockSpec(memory_space=pl.ANY)],
        out_specs=pl.BlockSpec((tq, 128), lambda b, h, i: (b * nq + i, h)),
        out_shape=jax.ShapeDtypeStruct(mix.shape, mix.dtype),
        input_output_aliases={5: 0},
        scratch_shapes=[pltpu.VMEM((seq, 128), BF16), pltpu.VMEM((nq, 128, tq), BF16)],
        compiler_params=_cparams("parallel", "parallel", "arbitrary"), name="attn_prompt",
    )(u, u, u, lam_params, subln_g.reshape(-1, 1, 128), mix)


def _softplus(x):
    return jnp.maximum(x, 0.0) + jnp.log1p(jnp.exp(-jnp.abs(x)))


def _silu(x):
    return x * jax.nn.sigmoid(x)


def _ssd_prompt_kernel(xs_ref, bc_ref, z_ref, dt_ref, cw_ref, cb_ref, dtb_ref, alog_ref, dsk_ref, ng_ref, mix_ref,
                       y_ref, st_ref, conv_ref, xp_ref, h_ref):
    del mix_ref
    c = pl.program_id(1)
    nc = pl.num_programs(1)
    q = SSD_CHUNK
    d_ssd = N_HEADS_S * 64

    @pl.when(c == 0)
    def _():
        xp_ref[0:8, :] = jnp.zeros((8, xp_ref.shape[1]), F32)
        h_ref[...] = jnp.zeros_like(h_ref)

    xp_ref[8:8 + q, 0:d_ssd] = xs_ref[...]
    xp_ref[8:8 + q, d_ssd:] = bc_ref[...]
    cw = cw_ref[...]
    xc = cb_ref[...] + cw[3:4, :] * xp_ref[8:8 + q, :]
    for j in range(1, CONV_K):
        xc = xc + cw[3 - j:4 - j, :] * xp_ref[8 - j:8 - j + q, :]
    xp_ref[0:8, :] = xp_ref[q:q + 8, :]
    xc = _silu(xc)
    xs = xc[:, :d_ssd]

    head_lane = lax.broadcasted_iota(jnp.int32, (q, 128), 1) < N_HEADS_S
    dt_raw = jnp.where(head_lane, dt_ref[...], 0.0)
    dt = _softplus(dt_raw + dtb_ref[...])
    a_neg = -jnp.exp(alog_ref[...])
    da = dt * a_neg
    ri = lax.broadcasted_iota(jnp.int32, (q, q), 0)
    ci = lax.broadcasted_iota(jnp.int32, (q, q), 1)
    causal = ci <= ri
    tril = jnp.where(causal, 1.0, 0.0).astype(F32)
    a_cs = jnp.dot(tril, da, preferred_element_type=F32, precision=lax.Precision.HIGHEST)
    a_cs_t = a_cs.T
    a_last = a_cs[q - 1:q, :]
    e_cs = jnp.exp(a_cs)
    e_end = jnp.exp(a_last - a_cs)
    e_last = jnp.exp(a_last)
    lane = lax.broadcasted_iota(jnp.int32, (q, 128), 1)
    lo = lane < 64
    rsel = lax.broadcasted_iota(jnp.int32, (128, SSD_STATE), 0) < 64
    dims_nt = (((1,), (1,)), ((), ()))
    dims_tn = (((0,), (0,)), ((), ()))
    hpg = N_HEADS_S // SSD_GROUPS

    ys = []
    for g in range(SSD_GROUPS):
        bm_g = xc[:, d_ssd + g * SSD_STATE:d_ssd + (g + 1) * SSD_STATE].astype(BF16)
        cm_g = xc[:, d_ssd + (SSD_GROUPS + g) * SSD_STATE:d_ssd + (SSD_GROUPS + g + 1) * SSD_STATE].astype(BF16)
        cb = lax.dot_general(cm_g, bm_g, dims_nt, preferred_element_type=F32)
        for pr in range(hpg // 2):
            h0 = g * hpg + 2 * pr
            x_pair = xs[:, h0 * 64:h0 * 64 + 128]
            dt_pair = jnp.where(lo, dt[:, h0:h0 + 1], dt[:, h0 + 1:h0 + 2])
            xdt = x_pair * dt_pair
            y_pair = jnp.zeros((q, 128), F32)
            for k, keep in ((0, lo), (1, jnp.logical_not(lo))):
                hh = h0 + k
                seg = a_cs[:, hh:hh + 1] - a_cs_t[hh:hh + 1, :]
                decay = jnp.exp(jnp.where(causal, seg, -jnp.inf))
                mat = (cb * decay).astype(BF16)
                y_pair = y_pair + jnp.dot(mat, jnp.where(keep, xdt, 0.0).astype(BF16), preferred_element_type=F32)
            end_pair = jnp.where(lo, e_end[:, h0:h0 + 1], e_end[:, h0 + 1:h0 + 2])
            cs_pair = jnp.where(lo, e_cs[:, h0:h0 + 1], e_cs[:, h0 + 1:h0 + 2])
            h_prev = h_ref[h0 * 64:h0 * 64 + 128, :]
            y_off = lax.dot_general(cm_g, h_prev.astype(BF16), dims_nt, preferred_element_type=F32) * cs_pair
            st = lax.dot_general((xdt * end_pair).astype(BF16), bm_g, dims_tn, preferred_element_type=F32)
            dec = jnp.where(rsel, e_last[:, h0:h0 + 1], e_last[:, h0 + 1:h0 + 2])
            h_ref[h0 * 64:h0 * 64 + 128, :] = dec * h_prev + st
            dsk_pair = jnp.where(lo[0:1, :], dsk_ref[:, h0:h0 + 1], dsk_ref[:, h0 + 1:h0 + 2])
            ys.append(y_pair + y_off + dsk_pair * x_pair)
    y = jnp.concatenate(ys, axis=1)
    gz = y * _silu(z_ref[...])
    half = d_ssd // SSD_GROUPS
    outs = []
    for g in range(SSD_GROUPS):
        part = gz[:, g * half:(g + 1) * half]
        outs.append(part * lax.rsqrt(jnp.mean(part * part, axis=1, keepdims=True) + RMS_EPS))
    y_ref[...] = (jnp.concatenate(outs, axis=1) * ng_ref[...]).astype(y_ref.dtype)

    @pl.when(c == nc - 1)
    def _():
        st_ref[...] = h_ref[...]
        conv_ref[:, 0:d_ssd] = xs_ref[q - 8:q, :]
        conv_ref[:, d_ssd:] = bc_ref[q - 8:q, :]


def _pad_lanes(v):
    return jnp.pad(v.reshape(1, -1), ((0, 0), (0, 128 - v.shape[-1])))


def ssd_prompt(u, conv_w, conv_b, dt_bias, a_log, d_skip, norm_g, layer, batch, seq, mix):
    q = SSD_CHUNK
    nc = seq // q
    d_ssd = N_HEADS_S * 64
    d_bc = 2 * SSD_GROUPS * SSD_STATE
    cdim = d_ssd + d_bc
    d_attn = N_HEADS_A * 128
    z0, x0, bc0 = 3 * d_attn, 3 * d_attn + d_ssd, 3 * d_attn + 2 * d_ssd
    dt0 = bc0 + d_bc
    assert z0 % d_ssd == 0 and x0 % d_ssd == 0 and bc0 % d_bc == 0 and dt0 % 128 == 0
    assert u.shape[1] == dt0 + N_HEADS_S
    row = lambda b, c: (b * nc + c, 0)
    vec = lambda width: pl.BlockSpec((1, width), lambda b, c: (0, 0))
    return pl.pallas_call(
        _ssd_prompt_kernel, grid=(batch, nc),
        in_specs=[pl.BlockSpec((q, d_ssd), lambda b, c: (b * nc + c, x0 // d_ssd)),
                  pl.BlockSpec((q, d_bc), lambda b, c: (b * nc + c, bc0 // d_bc)),
                  pl.BlockSpec((q, d_ssd), lambda b, c: (b * nc + c, z0 // d_ssd)),
                  pl.BlockSpec((q, 128), lambda b, c: (b * nc + c, dt0 // 128)),
                  pl.BlockSpec((None, CONV_K, cdim), lambda b, c: (layer, 0, 0)),
                  vec(cdim), vec(128), vec(128), vec(128), vec(d_ssd), pl.BlockSpec(memory_space=pl.ANY)],
        out_specs=[pl.BlockSpec((q, d_ssd), lambda b, c: (b * nc + c, d_attn // d_ssd)),
                   pl.BlockSpec((None, N_HEADS_S * 64, SSD_STATE), lambda b, c: (b, 0, 0)),
                   pl.BlockSpec((None, 8, cdim), lambda b, c: (b, 0, 0))],
        out_shape=[jax.ShapeDtypeStruct(mix.shape, mix.dtype),
                   jax.ShapeDtypeStruct((batch, N_HEADS_S * 64, SSD_STATE), F32),
                   jax.ShapeDtypeStruct((batch, 8, cdim), F32)],
        input_output_aliases={10: 0},
        scratch_shapes=[pltpu.VMEM((q + 8, cdim), F32), pltpu.VMEM((N_HEADS_S * 64, SSD_STATE), F32)],
        compiler_params=_cparams("parallel", "arbitrary"), name="ssd_prompt",
    )(u, u, u, u, conv_w, conv_b[layer].reshape(1, -1), _pad_lanes(dt_bias[layer]),
      _pad_lanes(a_log[layer]), _pad_lanes(d_skip[layer]), norm_g[layer].reshape(1, -1), mix)


def _kv_heads_kernel(k0_ref, v0_ref, k1_ref, v1_ref, ko_ref, vo_ref):
    def emit(k_ref, v_ref):
        for h in range(N_HEADS_A):
            ko_ref[:, h, :] = k_ref[:, h * 128:(h + 1) * 128]
            vo_ref[:, h, :] = v_ref[:, h * 128:(h + 1) * 128]

    @pl.when(pl.program_id(0) == 0)
    def _():
        emit(k0_ref, v0_ref)

    @pl.when(pl.program_id(0) == 1)
    def _():
        emit(k1_ref, v1_ref)


def kv_prompt_outputs(us, rows, tl=512):
    assert len(us) == 2 and rows % tl == 0
    nb = rows // tl
    d_attn = N_HEADS_A * 128

    def col(which, layer):
        idle = nb - 1 if layer == 0 else 0
        return pl.BlockSpec((tl, d_attn), lambda l, i: (jnp.where(l == layer, i, idle), which))

    out = pl.BlockSpec((None, tl, N_HEADS_A, 128), lambda l, i: (l, i, 0, 0))
    shape = jax.ShapeDtypeStruct((2, rows, N_HEADS_A, 128), F32)
    return pl.pallas_call(
        _kv_heads_kernel, grid=(2, nb),
        in_specs=[col(1, 0), col(2, 0), col(1, 1), col(2, 1)], out_specs=[out, out], out_shape=[shape, shape],
        compiler_params=_cparams("arbitrary", "arbitrary"), name="kv_heads",
    )(us[0], us[0], us[1], us[1])


def _attn_decode_kernel(n_pages, page, lam_init, pt_ref, q_ref, kn_ref, vn_ref, lp_ref, g_ref, *refs):
    k_refs, v_refs = refs[:n_pages], refs[n_pages:2 * n_pages]
    o_ref, s_ref = refs[2 * n_pages], refs[2 * n_pages + 1]
    lam = _lambda_value(lp_ref[...], lam_init)
    q = q_ref[...] * (QK_DIM ** -0.5 * LOG2E)
    lo = lax.broadcasted_iota(jnp.int32, q.shape, 1) < QK_DIM
    slope = jnp.exp2(-(lax.broadcasted_iota(jnp.int32, q.shape, 0) + 1).astype(F32)) * LOG2E
    tok_bias = slope[None] * lax.broadcasted_iota(jnp.int32, (page, 1, 1), 0).astype(F32)
    past = n_pages * page

    ri = lax.broadcasted_iota(jnp.int32, (128, 128), 0) < QK_DIM
    ci = lax.broadcasted_iota(jnp.int32, (128, 128), 1) < QK_DIM
    half_sum = jnp.where(ri == ci, 1.0, 0.0).astype(BF16)

    def packed_scores(k):
        prod = (k * q).reshape(-1, 128).astype(BF16)
        return jnp.dot(prod, half_sum, preferred_element_type=F32).reshape(k.shape)

    s_new = packed_scores(kn_ref[...])
    m = s_new
    page_bias = [slope * float(past - p * page) for p in range(n_pages)]
    for p in range(n_pages):
        s = packed_scores(k_refs[p][...]) + tok_bias
        s_ref[p * page:(p + 1) * page] = s
        m = jnp.maximum(m, jnp.max(s, axis=0) - page_bias[p])
    e_new = jnp.exp2(s_new - m)
    l = e_new
    for p in range(n_pages):
        e = jnp.exp2(s_ref[p * page:(p + 1) * page] - (m + page_bias[p])[None])
        s_ref[p * page:(p + 1) * page] = e
        l = l + jnp.sum(e, axis=0)
    r = 1.0 / l
    coef = jnp.where(lo, r, -lam * r)
    w_new = e_new * coef
    acc = (w_new + pltpu.roll(w_new, QK_DIM, 1)) * vn_ref[...]
    for p in range(n_pages):
        w = s_ref[p * page:(p + 1) * page] * coef[None]
        w = w + pltpu.roll(w, QK_DIM, 2)
        acc = acc + jnp.sum(w * v_refs[p][...], axis=0)
    o = acc * lax.rsqrt(jnp.mean(acc * acc, axis=-1, keepdims=True) + RMS_EPS)
    o_ref[...] = o * g_ref[...] * (1.0 - lam_init)


def attn_decode(q, k_new, v_new, cache_k, cache_v, page_table, lam_params, subln_g, layer, lam_init):
    db, n_pages = page_table.shape
    page = cache_k.shape[2]

    def kv_spec(j):
        return pl.BlockSpec((None, None, page, N_HEADS_A, 128), lambda b, pt: (layer, pt[b, j], 0, 0, 0))

    tok = pl.BlockSpec((None, N_HEADS_A, 128), lambda b, pt: (b, 0, 0))
    in_specs = [tok, tok, tok,
                pl.BlockSpec((None, 4, QK_DIM), lambda b, pt: (layer, 0, 0)),
                pl.BlockSpec((None, 1, 128), lambda b, pt: (layer, 0, 0))]
    in_specs += [kv_spec(j) for j in range(n_pages)] * 2
    return pl.pallas_call(
        functools.partial(_attn_decode_kernel, n_pages, page, lam_init),
        grid_spec=pltpu.PrefetchScalarGridSpec(
            num_scalar_prefetch=1, grid=(db,), in_specs=in_specs, out_specs=tok,
            scratch_shapes=[pltpu.VMEM((n_pages * page, N_HEADS_A, 128), F32)]),
        out_shape=jax.ShapeDtypeStruct((db, N_HEADS_A, 128), F32),
        compiler_params=_cparams("parallel"), name="attn_decode",
    )(page_table, q, k_new, v_new, lam_params, subln_g.reshape(-1, 1, 128), *([cache_k] * n_pages),
      *([cache_v] * n_pages))


def _ssd_decode_kernel(layer, first, *refs):
    (xbc_ref, z_ref, dt_ref, cst_ref, ssm_ref, cw_ref, cb_ref, dtb_ref, alog_ref, dsk_ref, ng_ref) = refs[:11]
    y_ref, ssm_out_ref, conv_out_ref = refs[-3:]
    if first:
        for other in range(ssm_out_ref.shape[0]):
            if other != layer:
                ssm_out_ref[other] = jnp.zeros(ssm_out_ref.shape[1:], F32)
        ssm_out_ref = ssm_out_ref.at[layer]
    for s in range(xbc_ref.shape[0]):
        _ssd_decode_one(xbc_ref.at[s], z_ref.at[s], dt_ref.at[s], cst_ref.at[s], ssm_ref.at[s], cw_ref, cb_ref, dtb_ref,
                        alog_ref, dsk_ref, ng_ref, y_ref.at[s], ssm_out_ref.at[s], conv_out_ref.at[s])


def _ssd_decode_one(xbc_ref, z_ref, dt_ref, cst_ref, ssm_ref, cw_ref, cb_ref, dtb_ref, alog_ref, dsk_ref,
                    ng_ref, y_ref, ssm_out_ref, conv_out_ref):
    d_ssd = N_HEADS_S * 64
    hpg = N_HEADS_S // SSD_GROUPS
    xnew = xbc_ref[...]
    cst = cst_ref[...]
    cw = cw_ref[...]
    xc = cb_ref[...] + cw[3:4, :] * xnew
    for j in range(CONV_K - 1):
        xc = xc + cw[j:j + 1, :] * cst[j:j + 1, :]
    xc = _silu(xc)
    conv_out_ref[0:2, :] = cst[1:3, :]
    conv_out_ref[2:3, :] = xnew

    dt = _softplus(dt_ref[...] + dtb_ref[...])
    dec = jnp.exp(dt * (-jnp.exp(alog_ref[...])))
    eye = lax.broadcasted_iota(jnp.int32, (128, 128), 0) == lax.broadcasted_iota(jnp.int32, (128, 128), 1)
    lo = lax.broadcasted_iota(jnp.int32, (1, 128), 1) < 64
    dims_nt = (((1,), (1,)), ((), ()))
    ys = []
    for pr in range(N_HEADS_S // 2):
        h0 = 2 * pr
        g = h0 // hpg
        x_pair = xc[:, h0 * 64:h0 * 64 + 128]
        b_g = xc[:, d_ssd + g * SSD_STATE:d_ssd + (g + 1) * SSD_STATE]
        c_g = xc[:, d_ssd + (SSD_GROUPS + g) * SSD_STATE:d_ssd + (SSD_GROUPS + g + 1) * SSD_STATE]
        dt_pair = jnp.where(lo, dt[:, h0:h0 + 1], dt[:, h0 + 1:h0 + 2])
        xdt_row = x_pair * dt_pair
        xdt_diag = jnp.where(eye, jnp.broadcast_to(xdt_row, (128, 128)), 0.0).astype(BF16)
        upd = jnp.dot(xdt_diag, jnp.broadcast_to(b_g, (128, SSD_STATE)).astype(BF16), preferred_element_type=F32)
        h_prev = ssm_ref[h0:h0 + 2]
        h_new = jnp.concatenate([jnp.broadcast_to(dec[:, h0 + k:h0 + k + 1], (64, SSD_STATE)) * h_prev[k]
                                 for k in range(2)], axis=0) + upd
        ssm_out_ref[h0:h0 + 2] = h_new.reshape(2, 64, SSD_STATE)
        y_rows = lax.dot_general(jnp.broadcast_to(c_g, (8, SSD_STATE)).astype(BF16), h_new.astype(BF16), dims_nt,
                                 preferred_element_type=F32)
        dsk_pair = jnp.where(lo, dsk_ref[:, h0:h0 + 1], dsk_ref[:, h0 + 1:h0 + 2])
        ys.append(y_rows[0:1, :] + dsk_pair * x_pair)
    y = jnp.concatenate(ys, axis=1)
    gz = y * _silu(z_ref[...])
    half = d_ssd // SSD_GROUPS
    outs = []
    for g in range(SSD_GROUPS):
        part = gz[:, g * half:(g + 1) * half]
        outs.append(part * lax.rsqrt(jnp.mean(part * part, axis=1, keepdims=True) + RMS_EPS))
    y_ref[...] = jnp.concatenate(outs, axis=1) * ng_ref[...]


def ssd_decode(xbc, z, dt_raw, state_conv, state_ssm, conv_w, conv_b, dt_bias, a_log, d_skip, norm_g, layer,
               ssm_stack=None):
    db, cdim = xbc.shape
    depth = state_ssm.shape[0]
    d_ssd = N_HEADS_S * 64
    bb = SSD_DECODE_SEQS
    assert db % bb == 0
    one = lambda width: pl.BlockSpec((bb, 1, width), lambda b: (b, 0, 0))
    vec = lambda width: pl.BlockSpec((1, width), lambda b: (0, 0))
    ssm_shape = state_ssm.shape[2:]
    first = ssm_stack is None
    in_specs = [one(cdim), one(d_ssd), one(128),
                pl.BlockSpec((None, bb, CONV_K - 1, cdim), lambda b: (layer, b, 0, 0)),
                pl.BlockSpec((None, bb) + ssm_shape, lambda b: (layer, b, 0, 0, 0)),
                pl.BlockSpec((None, CONV_K, cdim), lambda b: (layer, 0, 0)),
                vec(cdim), vec(128), vec(128), vec(128), vec(d_ssd)]
    args = [xbc.reshape(db, 1, cdim), z.reshape(db, 1, d_ssd), dt_raw.reshape(db, 1, 128), state_conv, state_ssm,
            conv_w, conv_b[layer].reshape(1, -1), _pad_lanes(dt_bias[layer]), _pad_lanes(a_log[layer]),
            _pad_lanes(d_skip[layer]), norm_g[layer].reshape(1, -1)]
    if first:
        ssm_out = pl.BlockSpec((depth, bb) + ssm_shape, lambda b: (0, b, 0, 0, 0))
        aliases = {}
    else:
        ssm_out = pl.BlockSpec((None, bb) + ssm_shape, lambda b: (layer, b, 0, 0, 0))
        in_specs.append(pl.BlockSpec(memory_space=pl.ANY))
        args.append(ssm_stack)
        aliases = {len(args) - 1: 1}
    return pl.pallas_call(
        functools.partial(_ssd_decode_kernel, layer, first), grid=(db // bb,),
        in_specs=in_specs,
        out_specs=[one(d_ssd), ssm_out, pl.BlockSpec((bb, CONV_K - 1, cdim), lambda b: (b, 0, 0))],
        out_shape=[jax.ShapeDtypeStruct((db, 1, d_ssd), F32),
                   jax.ShapeDtypeStruct((depth, db) + ssm_shape, F32),
                   jax.ShapeDtypeStruct((db, CONV_K - 1, cdim), F32)],
        input_output_aliases=aliases,
        compiler_params=_cparams("arbitrary"), name="ssd_decode",
    )(*args)


def _mem_attn_prompt_kernel(q_ref, k_ref, v_ref, init_ref, o_ref, kbf_ref, vbf_ref):
    del init_ref

    @pl.when(pl.program_id(1) == 0)
    def _():
        kbf_ref[...] = k_ref[...].astype(BF16)
        vbf_ref[...] = v_ref[...].astype(BF16)

    dh = q_ref.shape[1] // N_HEADS_MEM
    dims_nt = (((1,), (1,)), ((), ()))
    for h in range(N_HEADS_MEM):
        sl = slice(h * dh, (h + 1) * dh)
        s = lax.dot_general(q_ref[:, sl], kbf_ref[:, sl], dims_nt, preferred_element_type=F32) * (dh ** -0.5)
        e = jnp.exp(s - jnp.max(s, axis=1, keepdims=True))
        p = e / jnp.sum(e, axis=1, keepdims=True)
        o_ref[:, sl] = jnp.dot(p.astype(BF16), vbf_ref[:, sl], preferred_element_type=F32).astype(o_ref.dtype)


def mem_attn_prompt(qm, kv, batch, seq, n_mem, out_init, tq=512):
    d = qm.shape[1]
    nq = seq // tq
    return pl.pallas_call(
        _mem_attn_prompt_kernel, grid=(batch, nq),
        in_specs=[pl.BlockSpec((tq, d), lambda b, i: (b * nq + i, 0)),
                  pl.BlockSpec((n_mem, d), lambda b, i: (b, 0)),
                  pl.BlockSpec((n_mem, d), lambda b, i: (b, 1)),
                  pl.BlockSpec(memory_space=pl.ANY)],
        out_specs=pl.BlockSpec((tq, d), lambda b, i: (b * nq + i, 0)),
        out_shape=jax.ShapeDtypeStruct(out_init.shape, out_init.dtype),
        input_output_aliases={3: 0},
        scratch_shapes=[pltpu.VMEM((n_mem, d), BF16), pltpu.VMEM((n_mem, d), BF16)],
        compiler_params=_cparams("parallel", "arbitrary"), name="mem_attn_prompt",
    )(qm, kv, kv, out_init)


def _mem_attn_decode_kernel(q_ref, k_ref, v_ref, o_ref):
    for s in range(q_ref.shape[0]):
        q = q_ref[s]
        sc = jnp.sum(k_ref[s] * q[None], axis=-1, keepdims=True) * (q.shape[-1] ** -0.5)
        e = jnp.exp(sc - jnp.max(sc, axis=0, keepdims=True))
        p = e / jnp.sum(e, axis=0, keepdims=True)
        o_ref[s] = jnp.sum(p * v_ref[s], axis=0)


def mem_attn_decode(q, cache_mem_k, cache_mem_v, layer, bb=2):
    db = q.shape[0]
    assert db % bb == 0
    blk = cache_mem_k.shape[2:]
    tok = pl.BlockSpec((bb,) + q.shape[1:], lambda b: (b, 0, 0))
    kv = pl.BlockSpec((None, bb) + blk, lambda b: (layer, b, 0, 0, 0))
    return pl.pallas_call(
        _mem_attn_decode_kernel, grid=(db // bb,), in_specs=[tok, kv, kv], out_specs=tok,
        out_shape=jax.ShapeDtypeStruct(q.shape, F32),
        compiler_params=_cparams("parallel"), name="mem_attn_decode",
    )(q, cache_mem_k, cache_mem_v)


def _router_kernel(x_ref, w_ref, idx_ref, gate_ref):
    logits = jnp.dot(x_ref[...], w_ref[...], preferred_element_type=F32, precision=lax.Precision.HIGHEST)
    n_exp = logits.shape[1]
    lane = lax.broadcasted_iota(jnp.int32, logits.shape, 1)
    m1 = jnp.max(logits, axis=1, keepdims=True)
    i1 = jnp.min(jnp.where(logits == m1, lane, n_exp), axis=1, keepdims=True)
    rest = jnp.where(lane == i1, -jnp.inf, logits)
    m2 = jnp.max(rest, axis=1, keepdims=True)
    i2 = jnp.min(jnp.where(rest == m2, lane, n_exp), axis=1, keepdims=True)
    e2 = jnp.exp(m2 - m1)
    idx_ref[:, 0:1] = i1
    idx_ref[:, 1:2] = i2
    gate_ref[:, 0:1] = 1.0 / (1.0 + e2)
    gate_ref[:, 1:2] = e2 / (1.0 + e2)


def router_top2(x, w_router, bm):
    m, d = x.shape
    n_exp = w_router.shape[-1]
    row = lambda w: pl.BlockSpec((bm, w), lambda i: (i, 0))
    return pl.pallas_call(
        _router_kernel, grid=(m // bm,),
        in_specs=[row(d), pl.BlockSpec((d, n_exp), lambda i: (0, 0))],
        out_specs=[row(TOP_K), row(TOP_K)],
        out_shape=[jax.ShapeDtypeStruct((m, TOP_K), jnp.int32), jax.ShapeDtypeStruct((m, TOP_K), F32)],
        compiler_params=_cparams("parallel"), name="router",
    )(x, w_router)


def _moe_plan(idx, n_exp, bm, n_tiles):
    e_flat = idx.reshape(-1)
    onehot = (e_flat[:, None] == jnp.arange(n_exp, dtype=jnp.int32)[None, :]).astype(jnp.int32)
    csum = jnp.cumsum(onehot, axis=0)
    counts = csum[-1]
    padded = ((counts + bm - 1) // bm) * bm
    gend = jnp.cumsum(padded)
    gstart = gend - padded
    dest = jnp.sum(onehot * (gstart[None, :] + csum - 1), axis=1).astype(jnp.int32)
    tile_start = jnp.arange(n_tiles, dtype=jnp.int32) * bm
    tile_expert = jnp.sum((tile_start[:, None] >= gend[None, :]).astype(jnp.int32), axis=1)
    tile_expert = jnp.minimum(tile_expert, n_exp - 1).astype(jnp.int32)
    n_valid = (gend[-1] // bm).astype(jnp.int32).reshape(1)
    ids = jnp.arange(n_exp, dtype=jnp.int32)
    later = jnp.logical_and(ids[None, :] > ids[:, None], (counts > 0)[None, :])
    first = jnp.min(jnp.where(counts > 0, ids, n_exp))
    nxt = jnp.min(jnp.where(later, ids[None, :], n_exp), axis=1)
    wrap = (nxt == n_exp).astype(jnp.int32)
    nxt = jnp.where(nxt == n_exp, first, nxt).astype(jnp.int32)
    return dest, (tile_expert, nxt[tile_expert], wrap[tile_expert], n_valid)


def _row_copy(src_ref, src_row, dst_ref, dst_row, sem):
    return pltpu.make_async_copy(src_ref.at[pl.ds(src_row, 1), :], dst_ref.at[pl.ds(dst_row, 1), :], sem)


def _moe_scatter_kernel(bm, dest_ref, x_ref, xs_in_ref, xs_ref, sem):
    del xs_in_ref
    base = pl.program_id(0) * bm * TOP_K

    def start(r, _):
        for k in range(TOP_K):
            _row_copy(x_ref, r, xs_ref, dest_ref[base + r * TOP_K + k], sem).start()
        return 0

    def wait(r, _):
        for k in range(TOP_K):
            _row_copy(x_ref, 0, xs_ref, 0, sem).wait()
        return 0

    lax.fori_loop(0, bm, start, 0)
    lax.fori_loop(0, bm, wait, 0)


def moe_scatter(x, dest, n_rows, bm):
    m, d = x.shape
    zeros = jnp.zeros((n_rows, d), x.dtype)
    return pl.pallas_call(
        functools.partial(_moe_scatter_kernel, bm),
        grid_spec=pltpu.PrefetchScalarGridSpec(
            num_scalar_prefetch=1, grid=(m // bm,),
            in_specs=[pl.BlockSpec((bm, d), lambda i, dest: (i, 0)), pl.BlockSpec(memory_space=pl.ANY)],
            out_specs=pl.BlockSpec(memory_space=pl.ANY),
            scratch_shapes=[pltpu.SemaphoreType.DMA(())]),
        out_shape=jax.ShapeDtypeStruct((n_rows, d), x.dtype),
        input_output_aliases={2: 0},
        compiler_params=_cparams("arbitrary"), name="moe_scatter",
    )(dest, x, zeros)


def _expert_changed(te_ref, i):
    return jnp.logical_or(i == 0, te_ref[i] != te_ref[jnp.maximum(i - 1, 0)])


def _weight_block_copies(w_hbm_refs, stage_ref, sem, e, j, bn):
    col = pl.multiple_of(j * bn, bn)
    return [pltpu.make_async_copy(w.at[0, e, :, pl.ds(col, bn)], stage_ref.at[k], sem.at[k])
            for k, w in enumerate(w_hbm_refs)]


def _gmm_weights_step(te_ref, nx_ref, wrap_ref, nv_ref, w_hbm_refs, stage_ref, wbf_ref, sem, bn):
    j, i = pl.program_id(0), pl.program_id(1)

    @pl.when(jnp.logical_and(j == 0, i == 0))
    def _():
        for c in _weight_block_copies(w_hbm_refs, stage_ref, sem, te_ref[0], 0, bn):
            c.start()

    @pl.when(jnp.logical_and(i < nv_ref[0], _expert_changed(te_ref, i)))
    def _():
        for c in _weight_block_copies(w_hbm_refs, stage_ref, sem, te_ref[i], j, bn):
            c.wait()
        wbf_ref[...] = stage_ref[...].astype(BF16)
        nj = j + wrap_ref[i]

        @pl.when(nj < pl.num_programs(0))
        def _():
            for c in _weight_block_copies(w_hbm_refs, stage_ref, sem, nx_ref[i], nj, bn):
                c.start()


def _gmm_up_kernel(bn, te_ref, nx_ref, wrap_ref, nv_ref, x_ref, wg_ref, wu_ref, o_ref, stage_ref, wbf_ref, sem):
    i = pl.program_id(1)
    _gmm_weights_step(te_ref, nx_ref, wrap_ref, nv_ref, (wg_ref, wu_ref), stage_ref, wbf_ref, sem, bn)

    @pl.when(i < nv_ref[0])
    def _():
        x = x_ref[...].astype(BF16)
        g = jnp.dot(x, wbf_ref[0], preferred_element_type=F32)
        u = jnp.dot(x, wbf_ref[1], preferred_element_type=F32)
        o_ref[...] = (g * jax.nn.sigmoid(g) * u).astype(o_ref.dtype)

    @pl.when(i >= nv_ref[0])
    def _():
        o_ref[...] = jnp.zeros_like(o_ref)


def gmm_up(xs, wg, wu, plan, bm, bn):
    rows, k = xs.shape
    f = wg.shape[-1]
    hbm = pl.BlockSpec(memory_space=pl.ANY)
    return pl.pallas_call(
        functools.partial(_gmm_up_kernel, bn),
        grid_spec=pltpu.PrefetchScalarGridSpec(
            num_scalar_prefetch=4, grid=(f // bn, rows // bm),
            in_specs=[pl.BlockSpec((bm, k), lambda j, i, *_: (i, 0)), hbm, hbm],
            out_specs=pl.BlockSpec((bm, bn), lambda j, i, *_: (i, j)),
            scratch_shapes=[pltpu.VMEM((2, k, bn), F32), pltpu.VMEM((2, k, bn), BF16),
                            pltpu.SemaphoreType.DMA((2,))]),
        out_shape=jax.ShapeDtypeStruct((rows, f), BF16),
        compiler_params=_cparams("arbitrary", "arbitrary"), name="gmm_up",
    )(*plan, xs, wg, wu)


def _gmm_down_kernel(bn, te_ref, nx_ref, wrap_ref, nv_ref, a_ref, w_ref, o_ref, stage_ref, wbf_ref, sem):
    i = pl.program_id(1)
    _gmm_weights_step(te_ref, nx_ref, wrap_ref, nv_ref, (w_ref,), stage_ref, wbf_ref, sem, bn)

    @pl.when(i < nv_ref[0])
    def _():
        o_ref[...] = jnp.dot(a_ref[...], wbf_ref[0], preferred_element_type=F32)

    @pl.when(i >= nv_ref[0])
    def _():
        o_ref[...] = jnp.zeros_like(o_ref)


def gmm_down(a, wd, plan, bm, bn):
    rows, f = a.shape
    d = wd.shape[-1]
    return pl.pallas_call(
        functools.partial(_gmm_down_kernel, bn),
        grid_spec=pltpu.PrefetchScalarGridSpec(
            num_scalar_prefetch=4, grid=(d // bn, rows // bm),
            in_specs=[pl.BlockSpec((bm, f), lambda j, i, *_: (i, 0)), pl.BlockSpec(memory_space=pl.ANY)],
            out_specs=pl.BlockSpec((bm, bn), lambda j, i, *_: (i, j)),
            scratch_shapes=[pltpu.VMEM((1, f, bn), F32), pltpu.VMEM((1, f, bn), BF16),
                            pltpu.SemaphoreType.DMA((1,))]),
        out_shape=jax.ShapeDtypeStruct((rows, d), F32),
        compiler_params=_cparams("arbitrary", "arbitrary"), name="gmm_down",
    )(*plan, a, wd)


def _moe_combine_kernel(bm, alpha, dest_ref, x_ref, gate_ref, g_ref, b_ref, ys_ref, o_ref, obf_ref, buf_ref, sem):
    base = pl.program_id(0) * bm * TOP_K

    def start(r, _):
        for k in range(TOP_K):
            _row_copy(ys_ref, dest_ref[base + r * TOP_K + k], buf_ref.at[k], r, sem).start()
        return 0

    def wait(r, _):
        for k in range(TOP_K):
            _row_copy(ys_ref, 0, buf_ref.at[k], 0, sem).wait()
        return 0

    lax.fori_loop(0, bm, start, 0)
    lax.fori_loop(0, bm, wait, 0)
    f = gate_ref[:, 0:1] * buf_ref[0]
    for k in range(1, TOP_K):
        f = f + gate_ref[:, k:k + 1] * buf_ref[k]
    y = _ln_rows(alpha * x_ref[...] + f, g_ref[...], b_ref[...])
    o_ref[...] = y
    obf_ref[...] = y.astype(BF16)


def moe_combine(x, gates, ys, dest, g, b, alpha, bm):
    m, d = x.shape
    row = lambda w: pl.BlockSpec((bm, w), lambda i, dest: (i, 0))
    vec = pl.BlockSpec((1, d), lambda i, dest: (0, 0))
    return pl.pallas_call(
        functools.partial(_moe_combine_kernel, bm, alpha),
        grid_spec=pltpu.PrefetchScalarGridSpec(
            num_scalar_prefetch=1, grid=(m // bm,),
            in_specs=[row(d), row(TOP_K), vec, vec, pl.BlockSpec(memory_space=pl.ANY)],
            out_specs=[row(d), row(d)],
            scratch_shapes=[pltpu.VMEM((TOP_K, bm, d), F32), pltpu.SemaphoreType.DMA(())]),
        out_shape=[jax.ShapeDtypeStruct((m, d), F32), jax.ShapeDtypeStruct((m, d), BF16)],
        compiler_params=_cparams("arbitrary"), name="moe_combine",
    )(dest, x, gates, g.reshape(1, d), b.reshape(1, d), ys)


def moe_ffn_ln(x, w_router, wg, wu, wd, g, b, alpha, moe_layer):
    m, d = x.shape
    n_exp = w_router.shape[-1]
    bm_tok = _row_block(m, 640)
    bm = 512
    n_tiles = -(-(m * TOP_K + n_exp * (bm - 1)) // bm)
    idx, gates = router_top2(x, w_router[moe_layer], bm_tok)
    dest, plan = _moe_plan(idx, n_exp, bm, n_tiles)
    xs = moe_scatter(x, dest, n_tiles * bm, bm_tok)
    a = gmm_up(xs, wg[moe_layer:moe_layer + 1], wu[moe_layer:moe_layer + 1], plan, bm, min(1024, wg.shape[-1]))
    ys = gmm_down(a, wd[moe_layer:moe_layer + 1], plan, bm, 512)
    return moe_combine(x, gates, ys, dest, g, b, alpha, bm_tok)


def kernel(x_prompt, x_sample, mem_prompt, cache_k, cache_v, cache_mem_k, cache_mem_v, state_ssm, state_conv, page_table, ln_in_g, ln_in_b, w_in, conv_w, conv_b, dt_bias, a_log, d_skip, ssd_norm_g, lam_params, subln_g, w_out, w_mem_q, w_mem_kv, w_mem_o, ln_g, ln_b, w_ff_gate, w_ff_up, w_ff_down, w_router, w_exp_gate, w_exp_up, w_exp_down):
    batch, seq, d = x_prompt.shape
    db = x_sample.shape[0]
    assert x_sample.shape[1] == 1
    depth = w_in.shape[0]
    n_mem = mem_prompt.shape[1]
    mp = batch * seq
    m = mp + db
    alpha = (2 * depth) ** 0.25
    d_attn = N_HEADS_A * 128
    d_ssd = N_HEADS_S * 64
    cdim = d_ssd + 2 * SSD_GROUPS * SSD_STATE
    n_main = 3 * d_attn + d_ssd + cdim
    bm = _row_block(m, 1664)
    bm_small = _row_block(m, 416)

    x_all = jnp.concatenate([x_prompt.reshape(mp, d), x_sample.reshape(db, d)], axis=0)
    x, x_bf = layer_norm_in(x_all, ln_in_g, ln_in_b)
    mem2d = mem_prompt.reshape(batch * n_mem, d)

    w_in_nk = jnp.swapaxes(w_in, 1, 2)
    outs = {k: [] for k in ("sp", "cp", "mkp", "mvp", "ks", "vs", "cs")}
    ssm_s = None
    u_layers = []
    for l in range(depth):
        lam_init = 0.8 - 0.6 * math.exp(-0.3 * l)
        u = matmul(x_bf, w_in_nk, lead=(l,), bm=bm, bn=512, w_is_nk=True, name="mm_in")
        us = u[mp:]
        heads = lambda a: a.reshape(db, N_HEADS_A, 128)
        o_a_s = attn_decode(heads(us[:, :d_attn]), heads(us[:, d_attn:2 * d_attn]), heads(us[:, 2 * d_attn:3 * d_attn]),
                            cache_k, cache_v, page_table, lam_params, subln_g, l, lam_init)
        dt_s = jnp.pad(us[:, n_main:], ((0, 0), (0, 128 - N_HEADS_S)))
        y_s_s, ssm_s, conv_s = ssd_decode(us[:, n_main - cdim:n_main], us[:, 3 * d_attn:3 * d_attn + d_ssd], dt_s,
                                          state_conv, state_ssm, conv_w, conv_b, dt_bias, a_log, d_skip, ssd_norm_g, l, ssm_s)
        mix_s = jnp.concatenate([o_a_s.reshape(db, d_attn), y_s_s.reshape(db, d_ssd)], axis=1).astype(BF16)
        mix = rows_below_zeros(mix_s, m)
        mix = attn_prompt(u, lam_params, subln_g, l, batch, seq, lam_init, mix)
        mix, ssm_p, conv_p = ssd_prompt(u, conv_w, conv_b, dt_bias, a_log, d_skip, ssd_norm_g, l, batch, seq, mix)
        x, x_bf = matmul_add_ln(mix, w_out, x, ln_g[l, 0], ln_b[l, 0], alpha, lead=(l,), bm=bm_small, name="mm_out_ln")
        qm = matmul(x_bf, w_mem_q, lead=(l,), bm=bm, bn=512, out_dtype=BF16, name="mm_mem_q")
        kv = matmul(mem2d, w_mem_kv, lead=(l,), bm=_row_block(batch * n_mem, 512), bn=512, name="mm_mem_kv")
        c_s = mem_attn_decode(qm[mp:].astype(F32).reshape(db, N_HEADS_MEM, d // N_HEADS_MEM), cache_mem_k, cache_mem_v, l)
        c = rows_below_zeros(c_s.reshape(db, d).astype(BF16), m)
        c = mem_attn_prompt(qm, kv, batch, seq, n_mem, c)
        x, x_bf = matmul_add_ln(c, w_mem_o, x, ln_g[l, 1], ln_b[l, 1], alpha, lead=(l,), bm=bm_small, name="mm_mem_o_ln")
        if l % 2 == 0:
            act = swiglu_up(x_bf, w_ff_gate, w_ff_up, lead=(l // 2,), bm=bm, bn=512)
            a = matmul(act, w_ff_down, lead=(l // 2,), bm=bm_small, bn=512, name="mm_ff_down")
            x, x_bf = add_layer_norm(x, a, ln_g[l, 2], ln_b[l, 2], alpha)
        else:
            x, x_bf = moe_ffn_ln(x, w_router, w_exp_gate, w_exp_up, w_exp_down, ln_g[l, 2], ln_b[l, 2], alpha, l // 2)
        u_layers.append(u)
        outs["sp"].append(ssm_p.reshape(batch, N_HEADS_S, 64, SSD_STATE))
        outs["cp"].append(conv_p[:, 8 - (CONV_K - 1):, :])
        outs["mkp"].append(kv[:, :d].reshape(batch, n_mem, N_HEADS_MEM, d // N_HEADS_MEM))
        outs["mvp"].append(kv[:, d:].reshape(batch, n_mem, N_HEADS_MEM, d // N_HEADS_MEM))
        outs["ks"].append(us[:, d_attn:2 * d_attn].reshape(db, 1, N_HEADS_A, 128))
        outs["vs"].append(us[:, 2 * d_attn:3 * d_attn].reshape(db, 1, N_HEADS_A, 128))
        outs["cs"].append(conv_s)
    st = {k: jnp.stack(v) for k, v in outs.items()}
    kp, vp = kv_prompt_outputs(u_layers, mp)
    st["ss"] = ssm_s
    st["kp"] = kp.reshape(depth, batch, seq, N_HEADS_A, 128)
    st["vp"] = vp.reshape(depth, batch, seq, N_HEADS_A, 128)
    return (x[:mp].reshape(batch, seq, d), x[mp:].reshape(db, 1, d),
            st["kp"], st["vp"], st["sp"], st["cp"], st["mkp"], st["mvp"], st["ks"], st["vs"], st["ss"], st["cs"])
```
